```python
import math
import jax, jax.numpy as jnp
from jax import lax
import numpy as np

D_MODEL = 1024
BATCH = 2
SEQ = 16384
DEPTH = 1
DEC_BATCH = 128
DEC_SEQ = 4
PAST_LEN = 8192
PAGE_SIZE = 128

POOL_WINDOWS = (2, 4, 8, 16)
N_POOL_GROUPS = len(POOL_WINDOWS)
POOL_GROUP_WIDTH = D_MODEL // 8
POOL_WIDTH = N_POOL_GROUPS * POOL_GROUP_WIDTH
POOL_HIST = max(POOL_WINDOWS) - 1
N_HEADS = 8
N_KV_HEADS = 4
GROUP = N_HEADS // N_KV_HEADS
HEAD_DIM = D_MODEL // N_HEADS // 2
V_DIM = 2 * HEAD_DIM
ROT_DIM = HEAD_DIM // 4
ROPE_THETA = 500000.0
Q_WIDTH = N_HEADS * 2 * HEAD_DIM
K_WIDTH = N_KV_HEADS * 2 * HEAD_DIM
V_WIDTH = N_KV_HEADS * V_DIM
ATTN_WIDTH = N_HEADS * V_DIM
IN_WIDTH = POOL_WIDTH + Q_WIDTH + K_WIDTH + V_WIDTH + 2 * D_MODEL
Q_BLOCK = 128
D_FF = ((8 * D_MODEL + 3 * 256 - 1) // (3 * 256)) * 256
EPS = 1e-6
NEG = -1e30

kernel_name = "pool_diffattn_gated_hybrid_step"


def rmsnorm(x, g):
    xf = x.astype(jnp.float32)
    y = xf * lax.rsqrt(jnp.mean(xf * xf, axis=-1, keepdims=True) + EPS)
    return (y * g.astype(jnp.float32)).astype(x.dtype)


def rope_partial(x, pos):
    half = ROT_DIM // 2
    inv = ROPE_THETA ** (-jnp.arange(half, dtype=jnp.float32) * 2.0 / ROT_DIM)
    ang = pos.astype(jnp.float32)[:, None] * inv[None, :]
    ang = ang.reshape((1, ang.shape[0]) + (1,) * (x.ndim - 3) + (half,))
    cos, sin = jnp.cos(ang), jnp.sin(ang)
    xr = x[..., :ROT_DIM].astype(jnp.float32)
    x1, x2 = xr[..., :half], xr[..., half:]
    rot = jnp.concatenate([x1 * cos - x2 * sin, x2 * cos + x1 * sin], axis=-1).astype(x.dtype)
    return jnp.concatenate([rot, x[..., ROT_DIM:]], axis=-1)


def project_inputs(x, norm_g, w_in, pos):
    n, t, _ = x.shape
    z = rmsnorm(x, norm_g) @ w_in
    o1 = POOL_WIDTH
    o2 = o1 + Q_WIDTH
    o3 = o2 + K_WIDTH
    o4 = o3 + V_WIDTH
    o5 = o4 + D_MODEL
    u = z[..., :o1]
    q = rope_partial(z[..., o1:o2].reshape(n, t, N_KV_HEADS, GROUP, 2, HEAD_DIM), pos)
    k = rope_partial(z[..., o2:o3].reshape(n, t, N_KV_HEADS, 2, HEAD_DIM), pos)
    v = z[..., o3:o4].reshape(n, t, N_KV_HEADS, V_DIM)
    return u, q, k, v, z[..., o4:o5], z[..., o5:]


def pool_mix(u_ext, pos, w_group, scale):
    n, _, w_all = u_ext.shape
    t = pos.shape[0]
    uf = u_ext.astype(jnp.float32)
    c = jnp.concatenate([jnp.zeros((n, 1, w_all), jnp.float32), jnp.cumsum(uf, axis=1)], axis=1)
    end = c[:, POOL_HIST + 1:]
    means = []
    for g, w in enumerate(POOL_WINDOWS):
        sl = slice(g * POOL_GROUP_WIDTH, (g + 1) * POOL_GROUP_WIDTH)
        start = c[:, POOL_HIST + 1 - w: POOL_HIST + 1 - w + t, sl]
        cnt = jnp.minimum(pos + 1, w).astype(jnp.float32)[None, :, None]
        means.append((end[..., sl] - start) / cnt)
    pooled = (jnp.concatenate(means, axis=-1) - uf[:, POOL_HIST:]).astype(u_ext.dtype)
    pooled = pooled.reshape(n, t, N_POOL_GROUPS, POOL_GROUP_WIDTH)
    mixed = jnp.einsum('ntgc,gcd->ntgd', pooled, w_group).reshape(n, t, POOL_WIDTH)
    return mixed * scale


def diff_attend(q, keys, values, masks, lam):
    scale = HEAD_DIM ** -0.5
    scores = []
    for k, m in zip(keys, masks):
        s = jnp.einsum('nqhgcd,nkhcd->nhgcqk', q, k).astype(jnp.float32) * scale
        if m is not None:
            s = jnp.where(m, s, NEG)
        scores.append(s)
    p = jax.nn.softmax(jnp.concatenate(scores, axis=-1), axis=-1)
    a = p[:, :, :, 0] - lam * p[:, :, :, 1]
    out = None
    off = 0
    for v in values:
        L = v.shape[1]
        o = jnp.einsum('nhgqk,nkhe->nqhge', a[..., off:off + L].astype(v.dtype), v)
        out = o if out is None else out + o
        off += L
    return out


def prompt_attention(q, k, v, pos, lam):
    b, s = q.shape[0], q.shape[1]
    nb = s // Q_BLOCK
    qb = q.reshape((b, nb, Q_BLOCK) + q.shape[2:]).swapaxes(0, 1)
    pb = pos.reshape(nb, Q_BLOCK)

    def block(args):
        qblk, qpos = args
        return diff_attend(qblk, [k], [v], [pos[None, :] <= qpos[:, None]], lam)

    o = lax.map(block, (qb, pb))
    return o.swapaxes(0, 1).reshape(b, s, N_KV_HEADS, GROUP, V_DIM)


def merge_and_ffn(x, pool_out, attn_out, ga, gb, lam_init, subln_g, w_pool_proj, w_attn_proj,
                  w_out, norm_ffn, w_gate, w_up, w_down):
    n, t, _ = x.shape
    o = rmsnorm(attn_out.reshape(n, t, N_HEADS, V_DIM), subln_g) * (1.0 - lam_init)
    a_branch = pool_out @ w_pool_proj
    b_branch = o.reshape(n, t, ATTN_WIDTH) @ w_attn_proj
    mixed = jax.nn.sigmoid(ga) * a_branch + jax.nn.sigmoid(gb) * b_branch
    x = x + mixed @ w_out
    h = rmsnorm(x, norm_ffn)
    return x + (jax.nn.silu(h @ w_gate) * (h @ w_up)) @ w_down


def setup_inputs(seed: int = 0) -> dict:
    key = jax.random.key(seed)
    ks = jax.random.split(key, 24)
    n_pages = PAST_LEN // PAGE_SIZE
    n_used = DEC_BATCH * n_pages
    n_pool = (n_used * 5 + 3) // 4
    nrm = lambda k, shape, s: jax.random.normal(k, shape, jnp.float32) * s
    page_table = jax.random.permutation(ks[0], n_pool)[:n_used].reshape(DEC_BATCH, n_pages).astype(jnp.int32)
    return {
        "x_prompt": nrm(ks[1], (BATCH, SEQ, D_MODEL), 1.0),
        "x_sample": nrm(ks[2], (DEC_BATCH, DEC_SEQ, D_MODEL), 1.0),
        "cache_k": nrm(ks[3], (DEPTH, n_pool, PAGE_SIZE, N_KV_HEADS, 2, HEAD_DIM), 1.0),
        "cache_v": nrm(ks[4], (DEPTH, n_pool, PAGE_SIZE, N_KV_HEADS, V_DIM), 1.0),
        "state_pool": nrm(ks[5], (DEPTH, DEC_BATCH, POOL_HIST, POOL_WIDTH), 1.0),
        "page_table": page_table,
        "norm_mix": 1.0 + nrm(ks[6], (DEPTH, D_MODEL), 0.05),
        "w_in": nrm(ks[7], (DEPTH, D_MODEL, IN_WIDTH), D_MODEL ** -0.5),
        "w_pool_group": nrm(ks[8], (DEPTH, N_POOL_GROUPS, POOL_GROUP_WIDTH, POOL_GROUP_WIDTH), POOL_GROUP_WIDTH ** -0.5),
        "pool_scale": 1.0 + nrm(ks[9], (DEPTH, POOL_WIDTH), 0.1),
        "lambda_q1": nrm(ks[10], (DEPTH, HEAD_DIM), 0.1),
        "lambda_k1": nrm(ks[11], (DEPTH, HEAD_DIM), 0.1),
        "lambda_q2": nrm(ks[12], (DEPTH, HEAD_DIM), 0.1),
        "lambda_k2": nrm(ks[13], (DEPTH, HEAD_DIM), 0.1),
        "subln_g": 1.0 + nrm(ks[14], (DEPTH, V_DIM), 0.05),
        "w_pool_proj": nrm(ks[15], (DEPTH, POOL_WIDTH, D_MODEL), POOL_WIDTH ** -0.5),
        "w_attn_proj": nrm(ks[16], (DEPTH, ATTN_WIDTH, D_MODEL), ATTN_WIDTH ** -0.5),
        "w_out": nrm(ks[17], (DEPTH, D_MODEL, D_MODEL), D_MODEL ** -0.5),
        "norm_ffn": 1.0 + nrm(ks[18], (DEPTH, D_MODEL), 0.05),
        "w_gate": nrm(ks[19], (DEPTH, D_MODEL, D_FF), D_MODEL ** -0.5),
        "w_up": nrm(ks[20], (DEPTH, D_MODEL, D_FF), D_MODEL ** -0.5),
        "w_down": nrm(ks[21], (DEPTH, D_FF, D_MODEL), D_FF ** -0.5),
        "final_norm": 1.0 + nrm(ks[22], (D_MODEL,), 0.05),
    }


def reference(x_prompt, x_sample, cache_k, cache_v, state_pool, page_table, norm_mix, w_in,
              w_pool_group, pool_scale, lambda_q1, lambda_k1, lambda_q2, lambda_k2, subln_g,
              w_pool_proj, w_attn_proj, w_out, norm_ffn, w_gate, w_up, w_down, final_norm):
    n_pages = page_table.shape[1]
    past = n_pages * PAGE_SIZE
    pos_p = jnp.arange(SEQ, dtype=jnp.int32)
    pos_s = past + jnp.arange(DEC_SEQ, dtype=jnp.int32)
    causal_new = jnp.tril(jnp.ones((DEC_SEQ, DEC_SEQ), dtype=bool))
    xp, xs = x_prompt, x_sample
    kp_l, vp_l, pp_l, ks_l, vs_l, ps_l = [], [], [], [], [], []
    for l in range(DEPTH):
        lam_init = 0.8 - 0.6 * math.exp(-0.3 * l)
        f32 = jnp.float32
        lam = (jnp.exp(jnp.sum(lambda_q1[l].astype(f32) * lambda_k1[l].astype(f32)))
               - jnp.exp(jnp.sum(lambda_q2[l].astype(f32) * lambda_k2[l].astype(f32))) + lam_init)
        tail = (lam_init, subln_g[l], w_pool_proj[l], w_attn_proj[l], w_out[l], norm_ffn[l],
                w_gate[l], w_up[l], w_down[l])
        u, q, k, v, ga, gb = project_inputs(xp, norm_mix[l], w_in[l], pos_p)
        u_ext = jnp.concatenate([jnp.zeros((BATCH, POOL_HIST, POOL_WIDTH), u.dtype), u], axis=1)
        pool_out = pool_mix(u_ext, pos_p, w_pool_group[l], pool_scale[l])
        attn_out = prompt_attention(q, k, v, pos_p, lam)
        xp = merge_and_ffn(xp, pool_out, attn_out, ga, gb, *tail)
        kp_l.append(k)
        vp_l.append(v)
        pp_l.append(u_ext[:, -POOL_HIST:])
        u, q, k, v, ga, gb = project_inputs(xs, norm_mix[l], w_in[l], pos_s)
        u_ext = jnp.concatenate([state_pool[l].astype(u.dtype), u], axis=1)
        pool_out = pool_mix(u_ext, pos_s, w_pool_group[l], pool_scale[l])
        k_past = cache_k[l, page_table].reshape(DEC_BATCH, past, N_KV_HEADS, 2, HEAD_DIM)
        v_past = cache_v[l, page_table].reshape(DEC_BATCH, past, N_KV_HEADS, V_DIM)
        attn_out = diff_attend(q, [k_past, k], [v_past, v], [None, causal_new], lam)
        xs = merge_and_ffn(xs, pool_out, attn_out, ga, gb, *tail)
        ks_l.append(k)
        vs_l.append(v)
        ps_l.append(u_ext[:, -POOL_HIST:])
    y_prompt = rmsnorm(xp, final_norm)
    y_sample = rmsnorm(xs, final_norm)
    return (y_prompt, y_sample, jnp.stack(kp_l), jnp.stack(vp_l), jnp.stack(pp_l),
            jnp.stack(ks_l), jnp.stack(vs_l), jnp.stack(ps_l))
```

```python
import functools
import math

import jax
import jax.numpy as jnp
from jax import lax
from jax.experimental import pallas as pl
from jax.experimental.pallas import tpu as pltpu

D_MODEL = 1024
PAGE_SIZE = 128
POOL_WINDOWS = (2, 4, 8, 16)
N_POOL_GROUPS = len(POOL_WINDOWS)
POOL_GROUP_WIDTH = D_MODEL // 8
POOL_WIDTH = N_POOL_GROUPS * POOL_GROUP_WIDTH
POOL_HIST = max(POOL_WINDOWS) - 1
N_HEADS = 8
N_KV_HEADS = 4
GROUP = N_HEADS // N_KV_HEADS
HEAD_DIM = D_MODEL // N_HEADS // 2
V_DIM = 2 * HEAD_DIM
ROT_DIM = HEAD_DIM // 4
ROPE_THETA = 500000.0
Q_WIDTH = N_HEADS * 2 * HEAD_DIM
K_WIDTH = N_KV_HEADS * 2 * HEAD_DIM
V_WIDTH = N_KV_HEADS * V_DIM
ATTN_WIDTH = N_HEADS * V_DIM
D_FF = ((8 * D_MODEL + 3 * 256 - 1) // (3 * 256)) * 256
EPS = 1e-6
NEG = -1e30

LANES = 128
HIST_ROWS = 16
VMEM_LIMIT = 56 * 1024 * 1024

F32 = jnp.float32
BF16 = jnp.bfloat16

O_Q = POOL_WIDTH
O_K = O_Q + Q_WIDTH
O_V = O_K + K_WIDTH
O_G = O_V + V_WIDTH
IN_WIDTH = O_G + 2 * D_MODEL


def _const_spec(shape):
    zeros = (0,) * len(shape)
    return pl.BlockSpec(shape, lambda *_: zeros, pipeline_mode=pl.Buffered(1))


def _params(n_grid):
    return pltpu.CompilerParams(
        dimension_semantics=("arbitrary",) * n_grid, vmem_limit_bytes=VMEM_LIMIT)


def _rms(x, g):
    return x * lax.rsqrt(jnp.mean(x * x, axis=-1, keepdims=True) + EPS) * g


def _proj_kernel(x_ref, g_ref, w_ref, cos_ref, sa_ref, sb_ref,
                 u_ref, q_ref, kf_ref, kb_ref, vf_ref, vb_ref, gate_ref):
    xn = _rms(x_ref[...], g_ref[...]).astype(BF16)

    def seg(lo, width):
        return jnp.dot(xn, w_ref[:, lo:lo + width], preferred_element_type=F32)

    cos, sa, sb = cos_ref[...], sa_ref[...], sb_ref[...]

    def rope(z):
        return z * cos + pltpu.roll(z, LANES - ROT_DIM // 2, 1) * sa + pltpu.roll(z, ROT_DIM // 2, 1) * sb

    u_ref[...] = seg(0, POOL_WIDTH)
    zq = seg(O_Q, Q_WIDTH)
    for i in range(Q_WIDTH // LANES):
        sl = slice(i * LANES, (i + 1) * LANES)
        q_ref[:, sl] = (rope(zq[:, sl]) * (HEAD_DIM ** -0.5)).astype(BF16)
    zk = seg(O_K, K_WIDTH)
    for i in range(K_WIDTH // LANES):
        sl = slice(i * LANES, (i + 1) * LANES)
        kr = rope(zk[:, sl])
        kf_ref[:, sl] = kr
        kb_ref[:, sl] = kr.astype(BF16)
    zv = seg(O_V, V_WIDTH)
    vf_ref[...] = zv
    vb_ref[...] = zv.astype(BF16)
    gate_ref[...] = seg(O_G, 2 * D_MODEL)


def _proj(x2d, g, w_bf, cos, sa, sb, tm):
    n = x2d.shape[0]
    n_tab = cos.shape[0] // tm
    row = lambda i: (i, 0)
    tab = lambda i: (i % n_tab, 0)
    outs = [(POOL_WIDTH, F32), (Q_WIDTH, BF16), (K_WIDTH, F32), (K_WIDTH, BF16),
            (V_WIDTH, F32), (V_WIDTH, BF16), (2 * D_MODEL, F32)]
    return pl.pallas_call(
        _proj_kernel,
        grid=(n // tm,),
        in_specs=[pl.BlockSpec((tm, D_MODEL), row), _const_spec((1, D_MODEL)),
                  _const_spec((D_MODEL, IN_WIDTH)),
                  pl.BlockSpec((tm, LANES), tab), pl.BlockSpec((tm, LANES), tab),
                  pl.BlockSpec((tm, LANES), tab)],
        out_specs=[pl.BlockSpec((tm, w), row) for w, _ in outs],
        out_shape=[jax.ShapeDtypeStruct((n, w), d) for w, d in outs],
        compiler_params=_params(1),
        name="proj",
    )(x2d, g, w_bf, cos, sa, sb)


def _rope_tables(pos):
    half = ROT_DIM // 2
    inv = ROPE_THETA ** (-jnp.arange(half, dtype=F32) * 2.0 / ROT_DIM)
    ang = pos.astype(F32)[:, None] * inv[None, :]
    c, s = jnp.cos(ang), jnp.sin(ang)
    t = pos.shape[0]
    pad = jnp.zeros((t, HEAD_DIM - ROT_DIM), F32)
    cos64 = jnp.concatenate([c, c, jnp.ones((t, HEAD_DIM - ROT_DIM), F32)], axis=1)
    sa64 = jnp.concatenate([-s, jnp.zeros_like(s), pad], axis=1)
    sb64 = jnp.concatenate([jnp.zeros_like(s), s, pad], axis=1)
    dup = lambda a: jnp.concatenate([a, a], axis=1)
    return dup(cos64), dup(sa64), dup(sb64)


def _pool_prompt_kernel(u_ref, hist_ref, o_ref, ext_ref, *, tm):
    i = pl.program_id(1)
    u = u_ref[...]
    ext_ref[0:HIST_ROWS, :] = jnp.where(i == 0, 0.0, hist_ref[...])
    ext_ref[HIST_ROWS:, :] = u
    pos = i * tm + lax.broadcasted_iota(jnp.int32, (tm, 1), 0)
    for g, w in enumerate(POOL_WINDOWS):
        sl = slice(g * POOL_GROUP_WIDTH, (g + 1) * POOL_GROUP_WIDTH)
        tot = u[:, sl]
        for k in range(1, w):
            tot = tot + ext_ref[pl.ds(HIST_ROWS - k, tm), sl]
        cnt = jnp.minimum(pos + 1, w).astype(F32)
        o_ref[:, sl] = tot / cnt - u[:, sl]


def _pool_prompt(u3, tm):
    b, s, _ = u3.shape
    per = tm // HIST_ROWS
    return pl.pallas_call(
        functools.partial(_pool_prompt_kernel, tm=tm),
        grid=(b, s // tm),
        in_specs=[pl.BlockSpec((None, tm, POOL_WIDTH), lambda bi, i: (bi, i, 0)),
                  pl.BlockSpec((None, HIST_ROWS, POOL_WIDTH),
                               lambda bi, i: (bi, jnp.maximum(i * per - 1, 0), 0))],
        out_specs=pl.BlockSpec((None, tm, POOL_WIDTH), lambda bi, i: (bi, i, 0)),
        out_shape=jax.ShapeDtypeStruct(u3.shape, F32),
        scratch_shapes=[pltpu.VMEM((HIST_ROWS + tm, POOL_WIDTH), F32)],
        compiler_params=_params(2),
        name="pool_prompt",
    )(u3, u3)


def _pool_sample_kernel(sp_ref, u_ref, o_ref, st_ref, *, past, n_new):
    def ext(r, sl):
        if r < POOL_HIST:
            return sp_ref[:, r * POOL_WIDTH + sl.start: r * POOL_WIDTH + sl.stop]
        r -= POOL_HIST
        return u_ref[:, r * POOL_WIDTH + sl.start: r * POOL_WIDTH + sl.stop]

    for t in range(n_new):
        for g, w in enumerate(POOL_WINDOWS):
            sl = slice(g * POOL_GROUP_WIDTH, (g + 1) * POOL_GROUP_WIDTH)
            cur = ext(POOL_HIST + t, sl)
            tot = cur
            for k in range(1, w):
                tot = tot + ext(POOL_HIST + t - k, sl)
            cnt = float(min(past + t + 1, w))
            o_ref[:, t * POOL_WIDTH + sl.start: t * POOL_WIDTH + sl.stop] = tot / cnt - cur
    keep = (POOL_HIST - n_new) * POOL_WIDTH
    st_ref[:, :keep] = sp_ref[:, n_new * POOL_WIDTH:]
    st_ref[:, keep:] = u_ref[...]


def _pool_sample(sp2, u2, past, n_new):
    b = sp2.shape[0]
    return pl.pallas_call(
        functools.partial(_pool_sample_kernel, past=past, n_new=n_new),
        out_shape=[jax.ShapeDtypeStruct(u2.shape, F32), jax.ShapeDtypeStruct(sp2.shape, F32)],
        compiler_params=pltpu.CompilerParams(vmem_limit_bytes=VMEM_LIMIT),
        name="pool_sample",
    )(sp2, u2)


def _lambda(lp_ref, lam_init):
    lp = lp_ref[...]
    a = jnp.sum(lp[0:1] * lp[1:2], axis=-1, keepdims=True)
    b = jnp.sum(lp[2:3] * lp[3:4], axis=-1, keepdims=True)
    return jnp.exp(a) - jnp.exp(b) + lam_init


def _attn_kernel(lp_ref, q_ref, k_ref, v_ref, o_ref, qs_ref, m_ref, l_ref, acc_ref,
                 *, tq, tk, lam_init):
    qi = pl.program_id(2)
    rows = 2 * GROUP * tq
    lane = lax.broadcasted_iota(jnp.int32, (tq, LANES), 1)
    zero = jnp.zeros((tq, LANES), BF16)
    for g in range(GROUP):
        qg = q_ref[:, g * LANES:(g + 1) * LANES]
        qs_ref[(2 * g) * tq:(2 * g + 1) * tq, :] = jnp.where(lane < HEAD_DIM, qg, zero)
        qs_ref[(2 * g + 1) * tq:(2 * g + 2) * tq, :] = jnp.where(lane >= HEAD_DIM, qg, zero)
    m_ref[...] = jnp.full((rows, LANES), NEG, F32)
    l_ref[...] = jnp.zeros((rows, LANES), F32)
    acc_ref[...] = jnp.zeros((rows, V_DIM), F32)

    def step(j, masked):
        start = pl.multiple_of(j * tk, tk)
        k = k_ref[pl.ds(start, tk), :]
        v = v_ref[pl.ds(start, tk), :]
        s = lax.dot_general(qs_ref[...], k, (((1,), (1,)), ((), ())),
                            preferred_element_type=F32)
        if masked:
            r = lax.broadcasted_iota(jnp.int32, (rows, tk), 0)
            qpos = qi * tq + lax.rem(r, tq)
            kpos = j * tk + lax.broadcasted_iota(jnp.int32, (rows, tk), 1)
            s = jnp.where(kpos <= qpos, s, NEG)
        m_prev = m_ref[...]
        m_next = jnp.maximum(m_prev, jnp.max(s, axis=1, keepdims=True))
        p = jnp.exp(s - pltpu.repeat(m_next, tk // LANES, axis=1))
        alpha = jnp.exp(m_prev - m_next)
        l_ref[...] = alpha * l_ref[...] + jnp.sum(p, axis=1, keepdims=True)
        acc_ref[...] = alpha * acc_ref[...] + jnp.dot(p.astype(BF16), v,
                                                      preferred_element_type=F32)
        m_ref[...] = m_next

    n_full = (qi * tq) // tk
    n_end = ((qi + 1) * tq + tk - 1) // tk

    def full_body(j, c):
        step(j, False)
        return c

    def diag_body(j, c):
        step(j, True)
        return c

    lax.fori_loop(0, n_full, full_body, 0)
    lax.fori_loop(n_full, n_end, diag_body, 0)

    lam = _lambda(lp_ref, lam_init)
    o = acc_ref[...] / l_ref[...]
    for g in range(GROUP):
        o0 = o[(2 * g) * tq:(2 * g + 1) * tq]
        o1 = o[(2 * g + 1) * tq:(2 * g + 2) * tq]
        o_ref[:, g * V_DIM:(g + 1) * V_DIM] = o0 - lam * o1


def _attn_prompt(lp, q3, k3, v3, lam_init, tq, tk):
    b, s, _ = q3.shape
    rows = 2 * GROUP * tq
    qw = GROUP * 2 * HEAD_DIM
    return pl.pallas_call(
        functools.partial(_attn_kernel, tq=tq, tk=tk, lam_init=lam_init),
        grid=(b, N_KV_HEADS, s // tq),
        in_specs=[_const_spec(lp.shape),
                  pl.BlockSpec((None, tq, qw), lambda bi, h, i: (bi, i, h)),
                  pl.BlockSpec((None, s, LANES), lambda bi, h, i: (bi, 0, h)),
                  pl.BlockSpec((None, s, LANES), lambda bi, h, i: (bi, 0, h))],
        out_specs=pl.BlockSpec((None, tq, GROUP * V_DIM), lambda bi, h, i: (bi, i, h)),
        out_shape=jax.ShapeDtypeStruct((b, s, ATTN_WIDTH), F32),
        scratch_shapes=[pltpu.VMEM((rows, LANES), BF16), pltpu.VMEM((rows, LANES), F32),
                        pltpu.VMEM((rows, LANES), F32), pltpu.VMEM((rows, V_DIM), F32)],
        compiler_params=_params(3),
        name="attn_prompt",
    )(lp, q3, k3, v3)


def _sattn_kernel(pt_ref, lp_ref, q_ref, kn_ref, vn_ref, *rest, n_pages_step, n_new, lam_init):
    del pt_ref
    k_refs = rest[:n_pages_step]
    v_refs = rest[n_pages_step:2 * n_pages_step]
    o_ref, qall_ref, new_ref, m_ref, l_ref, acc_ref = rest[2 * n_pages_step:]
    step_i = pl.program_id(1)
    hrows = 2 * GROUP * n_new
    rows = N_KV_HEADS * hrows

    def update(s, v_slabs):
        m_prev = m_ref[...]
        m_next = jnp.maximum(m_prev, jnp.max(s, axis=1, keepdims=True))
        p = jnp.exp(s - pltpu.repeat(m_next, s.shape[1] // LANES, axis=1))
        alpha = jnp.exp(m_prev - m_next)
        l_ref[...] = alpha * l_ref[...] + jnp.sum(p, axis=1, keepdims=True)
        p = p.astype(BF16)
        pv = None
        for n, vs in enumerate(v_slabs):
            d = jnp.dot(p[:, n * LANES:(n + 1) * LANES], vs, preferred_element_type=F32)
            pv = d if pv is None else pv + d
        acc_ref[...] = pltpu.repeat(alpha, V_WIDTH // LANES, axis=1) * acc_ref[...] + pv
        m_ref[...] = m_next

    def scores(k_slab):
        return lax.dot_general(qall_ref[...].astype(BF16), k_slab, (((1,), (1,)), ((), ())),
                               preferred_element_type=F32)

    @pl.when(step_i == 0)
    def _():
        qall_ref[...] = jnp.zeros(qall_ref.shape, F32)
        lane = lax.broadcasted_iota(jnp.int32, (GROUP * n_new, LANES), 1)
        zero = jnp.zeros((GROUP * n_new, LANES), F32)
        for h in range(N_KV_HEADS):
            qh = q_ref[h]
            r0 = h * hrows
            qall_ref[r0:r0 + GROUP * n_new, h * LANES:(h + 1) * LANES] = \
                jnp.where(lane < HEAD_DIM, qh, zero)
            qall_ref[r0 + GROUP * n_new:r0 + hrows, h * LANES:(h + 1) * LANES] = \
                jnp.where(lane >= HEAD_DIM, qh, zero)
        m_ref[...] = jnp.full(m_ref.shape, NEG, F32)
        l_ref[...] = jnp.zeros(l_ref.shape, F32)
        acc_ref[...] = jnp.zeros(acc_ref.shape, F32)
        new_ref[...] = jnp.zeros(new_ref.shape, F32)
        new_ref[0, 0:n_new, :] = kn_ref[...]
        new_ref[1, 0:n_new, :] = vn_ref[...]
        s = scores(new_ref[0].astype(BF16))
        t_row = lax.rem(lax.broadcasted_iota(jnp.int32, (rows, LANES), 0), n_new)
        t_key = lax.broadcasted_iota(jnp.int32, (rows, LANES), 1)
        s = jnp.where(t_key <= t_row, s, NEG)
        update(s, [new_ref[1].astype(BF16)])

    s = jnp.concatenate([scores(kr[...].astype(BF16)) for kr in k_refs], axis=1)
    update(s, [vr[...].astype(BF16) for vr in v_refs])

    @pl.when(step_i == pl.num_programs(1) - 1)
    def _():
        lam = _lambda(lp_ref, lam_init)
        half = GROUP * n_new
        for h in range(N_KV_HEADS):
            blk = acc_ref[h * hrows:(h + 1) * hrows, h * V_DIM:(h + 1) * V_DIM]
            o = blk / l_ref[h * hrows:(h + 1) * hrows, :]
            o_ref[h] = o[:half] - lam * o[half:]


def _attn_sample(page_table, lp, q4, kn3, vn3, ck3, cv3, lam_init, n_pages_step):
    b, n_pages = page_table.shape
    n_new = kn3.shape[1]
    hrows = 2 * GROUP * n_new
    rows = N_KV_HEADS * hrows
    steps = n_pages // n_pages_step

    def page_spec(n):
        return pl.BlockSpec((None, PAGE_SIZE, K_WIDTH),
                            lambda bi, si, pt: (pt[bi, si * n_pages_step + n], 0, 0))

    grid_spec = pltpu.PrefetchScalarGridSpec(
        num_scalar_prefetch=1,
        grid=(b, steps),
        in_specs=[pl.BlockSpec(lp.shape, lambda bi, si, pt: (0, 0)),
                  pl.BlockSpec((None,) + q4.shape[1:], lambda bi, si, pt: (bi, 0, 0, 0)),
                  pl.BlockSpec((None, n_new, K_WIDTH), lambda bi, si, pt: (bi, 0, 0)),
                  pl.BlockSpec((None, n_new, V_WIDTH), lambda bi, si, pt: (bi, 0, 0))]
                 + [page_spec(n) for n in range(n_pages_step)] * 2,
        out_specs=pl.BlockSpec((None,) + q4.shape[1:], lambda bi, si, pt: (bi, 0, 0, 0)),
        scratch_shapes=[pltpu.VMEM((rows, K_WIDTH), F32),
                        pltpu.VMEM((2, PAGE_SIZE, K_WIDTH), F32),
                        pltpu.VMEM((rows, LANES), F32), pltpu.VMEM((rows, LANES), F32),
                        pltpu.VMEM((rows, V_WIDTH), F32)])
    return pl.pallas_call(
        functools.partial(_sattn_kernel, n_pages_step=n_pages_step, n_new=n_new,
                          lam_init=lam_init),
        grid_spec=grid_spec,
        out_shape=jax.ShapeDtypeStruct(q4.shape, F32),
        compiler_params=_params(2),
        name="attn_sample",
    )(page_table, lp, q4, kn3, vn3, *([ck3] * n_pages_step), *([cv3] * n_pages_step))


def _merge_kernel(x_ref, pooled_ref, attn_ref, gate_ref, wpg_ref, ps_ref, sg_ref,
                  wpp_ref, wap_ref, wo_ref, o_ref, *, lam_init):
    pooled = pooled_ref[...].astype(BF16)
    mixed = [jnp.dot(pooled[:, g * POOL_GROUP_WIDTH:(g + 1) * POOL_GROUP_WIDTH], wpg_ref[g],
                     preferred_element_type=F32) for g in range(N_POOL_GROUPS)]
    pool_out = jnp.concatenate(mixed, axis=1) * ps_ref[...]
    a_branch = jnp.dot(pool_out.astype(BF16), wpp_ref[...], preferred_element_type=F32)
    sg = sg_ref[...]
    heads = []
    for h in range(N_HEADS):
        oh = attn_ref[:, h * V_DIM:(h + 1) * V_DIM]
        heads.append((_rms(oh, sg) * (1.0 - lam_init)).astype(BF16))
    b_branch = jnp.dot(jnp.concatenate(heads, axis=1), wap_ref[...], preferred_element_type=F32)
    ga = gate_ref[:, :D_MODEL]
    gb = gate_ref[:, D_MODEL:]
    mix = jax.nn.sigmoid(ga) * a_branch + jax.nn.sigmoid(gb) * b_branch
    o_ref[...] = x_ref[...] + jnp.dot(mix.astype(BF16), wo_ref[...], preferred_element_type=F32)


def _merge(x2d, pooled, attn, gates, wpg, ps, sg, wpp, wap, wo, lam_init, tm):
    n = x2d.shape[0]
    row = lambda i: (i, 0)
    return pl.pallas_call(
        functools.partial(_merge_kernel, lam_init=lam_init),
        grid=(n // tm,),
        in_specs=[pl.BlockSpec((tm, D_MODEL), row), pl.BlockSpec((tm, POOL_WIDTH), row),
                  pl.BlockSpec((tm, ATTN_WIDTH), row), pl.BlockSpec((tm, 2 * D_MODEL), row),
                  _const_spec(wpg.shape), _const_spec(ps.shape), _const_spec(sg.shape),
                  _const_spec(wpp.shape), _const_spec(wap.shape), _const_spec(wo.shape)],
        out_specs=pl.BlockSpec((tm, D_MODEL), row),
        out_shape=jax.ShapeDtypeStruct((n, D_MODEL), F32),
        compiler_params=_params(1),
        name="merge",
    )(x2d, pooled, attn, gates, wpg, ps, sg, wpp, wap, wo)


def _ffn_kernel(x_ref, nf_ref, wg_ref, wu_ref, wd_ref, fn_ref, o_ref):
    x = x_ref[...]
    h = _rms(x, nf_ref[...]).astype(BF16)
    gate = jnp.dot(h, wg_ref[...], preferred_element_type=F32)
    up = jnp.dot(h, wu_ref[...], preferred_element_type=F32)
    act = (jax.nn.silu(gate) * up).astype(BF16)
    y = x + jnp.dot(act, wd_ref[...], preferred_element_type=F32)
    o_ref[...] = _rms(y, fn_ref[...])


def _ffn(x2d, nf, wg, wu, wd, fn, tm):
    n = x2d.shape[0]
    row = lambda i: (i, 0)
    return pl.pallas_call(
        _ffn_kernel,
        grid=(n // tm,),
        in_specs=[pl.BlockSpec((tm, D_MODEL), row), _const_spec(nf.shape), _const_spec(wg.shape),
                  _const_spec(wu.shape), _const_spec(wd.shape), _const_spec(fn.shape)],
        out_specs=pl.BlockSpec((tm, D_MODEL), row),
        out_shape=jax.ShapeDtypeStruct((n, D_MODEL), F32),
        compiler_params=_params(1),
        name="ffn",
    )(x2d, nf, wg, wu, wd, fn)


def kernel(x_prompt, x_sample, cache_k, cache_v, state_pool, page_table, norm_mix, w_in,
           w_pool_group, pool_scale, lambda_q1, lambda_k1, lambda_q2, lambda_k2, subln_g,
           w_pool_proj, w_attn_proj, w_out, norm_ffn, w_gate, w_up, w_down, final_norm):
    depth = w_in.shape[0]
    assert depth == 1, "single-layer step"
    batch, seq, _ = x_prompt.shape
    dec_batch, dec_seq, _ = x_sample.shape
    n_pages = page_table.shape[1]
    past = n_pages * PAGE_SIZE
    n_pool = cache_k.shape[1]
    l = 0
    lam_init = 0.8 - 0.6 * math.exp(-0.3 * l)

    tm = 512
    row2 = lambda a: a.reshape(1, -1)
    w_in_bf = w_in[l].astype(BF16)
    wpg = w_pool_group[l].astype(BF16)
    wpp = w_pool_proj[l].astype(BF16)
    wap = w_attn_proj[l].astype(BF16)
    wo = w_out[l].astype(BF16)
    wg = w_gate[l].astype(BF16)
    wu = w_up[l].astype(BF16)
    wd = w_down[l].astype(BF16)
    g_mix, ps, sg = row2(norm_mix[l]), row2(pool_scale[l]), row2(subln_g[l])
    nf, fn = row2(norm_ffn[l]), row2(final_norm)
    lp = jnp.stack([lambda_q1[l], lambda_k1[l], lambda_q2[l], lambda_k2[l]]).astype(F32)

    def tail(x2d, pooled, attn, gates, tile):
        x1 = _merge(x2d, pooled, attn, gates, wpg, ps, sg, wpp, wap, wo, lam_init, tile)
        return _ffn(x1, nf, wg, wu, wd, fn, tile)

    n_p = batch * seq
    xp = x_prompt.reshape(n_p, D_MODEL)
    tabs_p = _rope_tables(jnp.arange(seq, dtype=jnp.int32))
    u, q, kf, kb, vf, vb, gates = _proj(xp, g_mix, w_in_bf, *tabs_p, tm)
    u3 = u.reshape(batch, seq, POOL_WIDTH)
    pooled = _pool_prompt(u3, tm).reshape(n_p, POOL_WIDTH)
    attn = _attn_prompt(lp, q.reshape(batch, seq, Q_WIDTH), kb.reshape(batch, seq, K_WIDTH),
                        vb.reshape(batch, seq, V_WIDTH), lam_init, 256, 512)
    y_prompt = tail(xp, pooled, attn.reshape(n_p, ATTN_WIDTH), gates, tm).reshape(x_prompt.shape)
    k_prompt = kf.reshape(1, batch, seq, N_KV_HEADS, 2, HEAD_DIM)
    v_prompt = vf.reshape(1, batch, seq, N_KV_HEADS, V_DIM)
    pool_prompt = u3[:, seq - POOL_HIST:, :][None]

    n_s = dec_batch * dec_seq
    xs = x_sample.reshape(n_s, D_MODEL)
    pos_s = past + jnp.arange(dec_seq, dtype=jnp.int32)
    tabs_s = [jnp.tile(t, (dec_batch, 1)) for t in _rope_tables(pos_s)]
    u, q, kf, _, vf, _, gates = _proj(xs, g_mix, w_in_bf, *tabs_s, n_s)
    pooled, pool_state = _pool_sample(
        state_pool[l].reshape(dec_batch, POOL_HIST * POOL_WIDTH),
        u.reshape(dec_batch, dec_seq * POOL_WIDTH), past, dec_seq)
    q4 = q.reshape(dec_batch, dec_seq, N_KV_HEADS, GROUP, V_DIM).transpose(0, 2, 3, 1, 4)
    q4 = q4.reshape(dec_batch, N_KV_HEADS, GROUP * dec_seq, V_DIM).astype(F32)
    o4 = _attn_sample(page_table, lp, q4, kf.reshape(dec_batch, dec_seq, K_WIDTH),
                      vf.reshape(dec_batch, dec_seq, V_WIDTH),
                      cache_k.reshape(n_pool, PAGE_SIZE, K_WIDTH),
                      cache_v.reshape(n_pool, PAGE_SIZE, V_WIDTH), lam_init, 8)
    attn = o4.reshape(dec_batch, N_KV_HEADS, GROUP, dec_seq, V_DIM).transpose(0, 3, 1, 2, 4)
    y_sample = tail(xs, pooled.reshape(n_s, POOL_WIDTH), attn.reshape(n_s, ATTN_WIDTH), gates,
                    n_s).reshape(x_sample.shape)
    k_sample = kf.reshape(1, dec_batch, dec_seq, N_KV_HEADS, 2, HEAD_DIM)
    v_sample = vf.reshape(1, dec_batch, dec_seq, N_KV_HEADS, V_DIM)
    pool_sample = pool_state.reshape(1, dec_batch, POOL_HIST, POOL_WIDTH)

    return (y_prompt, y_sample, k_prompt, v_prompt, pool_prompt,
            k_sample, v_sample, pool_sample)
```

```python
import functools
import math

import jax
import jax.numpy as jnp
from jax import lax
from jax.experimental import pallas as pl
from jax.experimental.pallas import tpu as pltpu

D_MODEL = 1024
PAGE_SIZE = 128
POOL_WINDOWS = (2, 4, 8, 16)
N_POOL_GROUPS = len(POOL_WINDOWS)
POOL_GROUP_WIDTH = D_MODEL // 8
POOL_WIDTH = N_POOL_GROUPS * POOL_GROUP_WIDTH
POOL_HIST = max(POOL_WINDOWS) - 1
N_HEADS = 8
N_KV_HEADS = 4
GROUP = N_HEADS // N_KV_HEADS
HEAD_DIM = D_MODEL // N_HEADS // 2
V_DIM = 2 * HEAD_DIM
ROT_DIM = HEAD_DIM // 4
ROPE_THETA = 500000.0
Q_WIDTH = N_HEADS * 2 * HEAD_DIM
K_WIDTH = N_KV_HEADS * 2 * HEAD_DIM
V_WIDTH = N_KV_HEADS * V_DIM
ATTN_WIDTH = N_HEADS * V_DIM
D_FF = ((8 * D_MODEL + 3 * 256 - 1) // (3 * 256)) * 256
EPS = 1e-6
NEG = -1e30
Q_SCALE = HEAD_DIM ** -0.5 * math.log2(math.e)

LANES = 128
HIST_ROWS = 16
VMEM_LIMIT = 56 * 1024 * 1024

F32 = jnp.float32
BF16 = jnp.bfloat16

O_Q = POOL_WIDTH
O_K = O_Q + Q_WIDTH
O_V = O_K + K_WIDTH
O_G = O_V + V_WIDTH
IN_WIDTH = O_G + 2 * D_MODEL


def _const_spec(shape):
    zeros = (0,) * len(shape)
    return pl.BlockSpec(shape, lambda *_: zeros, pipeline_mode=pl.Buffered(1))


def _params(n_grid):
    return pltpu.CompilerParams(
        dimension_semantics=("arbitrary",) * n_grid, vmem_limit_bytes=VMEM_LIMIT)


def _rms(x, g):
    return x * lax.rsqrt(jnp.mean(x * x, axis=-1, keepdims=True) + EPS) * g


def _proj_kernel(x_ref, g_ref, w_ref, cos_ref, sa_ref, sb_ref,
                 u_ref, q_ref, kf_ref, kb_ref, vf_ref, vb_ref, gate_ref, *, k_transposed):
    tm = x_ref.shape[0]
    xn = _rms(x_ref[...], g_ref[...]).astype(BF16)

    def seg(lo, width):
        return jnp.dot(xn, w_ref[:, lo:lo + width], preferred_element_type=F32)

    cos, sa, sb = cos_ref[...], sa_ref[...], sb_ref[...]

    def rope(z):
        return z * cos + pltpu.roll(z, LANES - ROT_DIM // 2, 1) * sa + pltpu.roll(z, ROT_DIM // 2, 1) * sb

    u_ref[...] = seg(0, POOL_WIDTH)
    zq = seg(O_Q, Q_WIDTH)
    for i in range(Q_WIDTH // LANES):
        sl = slice(i * LANES, (i + 1) * LANES)
        q_ref[:, sl] = (rope(zq[:, sl]) * Q_SCALE).astype(BF16)
    zk = seg(O_K, K_WIDTH)
    for i in range(K_WIDTH // LANES):
        sl = slice(i * LANES, (i + 1) * LANES)
        kr = rope(zk[:, sl])
        if k_transposed:
            kf_ref[sl, :] = kr.T
        else:
            kf_ref[:, sl] = kr
        kb_ref[:, sl] = kr.astype(BF16)
    zv = seg(O_V, V_WIDTH)
    for h in range(N_KV_HEADS):
        vf_ref[pl.ds(h, tm, stride=N_KV_HEADS), :] = zv[:, h * V_DIM:(h + 1) * V_DIM]
        vb_ref[:, 2 * h * V_DIM:(2 * h + 1) * V_DIM] = zv[:, h * V_DIM:(h + 1) * V_DIM].astype(BF16)
        vb_ref[:, (2 * h + 1) * V_DIM:(2 * h + 2) * V_DIM] = jnp.ones((tm, V_DIM), BF16)
    gate_ref[...] = seg(O_G, 2 * D_MODEL)


def _proj(x3d, g, w_bf, cos, sa, sb, tm, k_transposed):
    b, t, _ = x3d.shape
    row = lambda bi, i: (bi, i, 0)
    tab = lambda bi, i: (i, 0)

    def rows(w, dt):
        return (pl.BlockSpec((None, tm, w), row), jax.ShapeDtypeStruct((b, t, w), dt))

    if k_transposed:
        k_out = (pl.BlockSpec((None, K_WIDTH, tm), lambda bi, i: (bi, 0, i)),
                 jax.ShapeDtypeStruct((b, K_WIDTH, t), F32))
    else:
        k_out = rows(K_WIDTH, F32)
    v_out = (pl.BlockSpec((None, tm * N_KV_HEADS, V_DIM), row),
             jax.ShapeDtypeStruct((b, t * N_KV_HEADS, V_DIM), F32))
    outs = [rows(POOL_WIDTH, F32), rows(Q_WIDTH, BF16), k_out, rows(K_WIDTH, BF16),
            v_out, rows(2 * V_WIDTH, BF16), rows(2 * D_MODEL, F32)]
    return pl.pallas_call(
        functools.partial(_proj_kernel, k_transposed=k_transposed),
        grid=(b, t // tm),
        in_specs=[pl.BlockSpec((None, tm, D_MODEL), row), _const_spec((1, D_MODEL)),
                  _const_spec((D_MODEL, IN_WIDTH)),
                  pl.BlockSpec((tm, LANES), tab), pl.BlockSpec((tm, LANES), tab),
                  pl.BlockSpec((tm, LANES), tab)],
        out_specs=[o[0] for o in outs],
        out_shape=[o[1] for o in outs],
        compiler_params=_params(2),
        name="proj",
    )(x3d, g, w_bf, cos, sa, sb)


def _rope_tables(pos):
    half = ROT_DIM // 2
    inv = ROPE_THETA ** (-jnp.arange(half, dtype=F32) * 2.0 / ROT_DIM)
    ang = pos.astype(F32)[:, None] * inv[None, :]
    c, s = jnp.cos(ang), jnp.sin(ang)
    t = pos.shape[0]
    pad = jnp.zeros((t, HEAD_DIM - ROT_DIM), F32)
    cos64 = jnp.concatenate([c, c, jnp.ones((t, HEAD_DIM - ROT_DIM), F32)], axis=1)
    sa64 = jnp.concatenate([-s, jnp.zeros_like(s), pad], axis=1)
    sb64 = jnp.concatenate([jnp.zeros_like(s), s, pad], axis=1)
    dup = lambda a: jnp.concatenate([a, a], axis=1)
    return dup(cos64), dup(sa64), dup(sb64)


def _pool_prompt_kernel(u_ref, hist_ref, o_ref, ext_ref, *, tm):
    i = pl.program_id(1)
    u = u_ref[...]
    ext_ref[0:HIST_ROWS, :] = jnp.where(i == 0, 0.0, hist_ref[...])
    ext_ref[HIST_ROWS:, :] = u
    pos = i * tm + lax.broadcasted_iota(jnp.int32, (tm, 1), 0)
    for g, w in enumerate(POOL_WINDOWS):
        sl = slice(g * POOL_GROUP_WIDTH, (g + 1) * POOL_GROUP_WIDTH)
        tot = u[:, sl]
        for k in range(1, w):
            tot = tot + ext_ref[pl.ds(HIST_ROWS - k, tm), sl]
        cnt = jnp.minimum(pos + 1, w).astype(F32)
        o_ref[:, sl] = tot / cnt - u[:, sl]


def _pool_prompt(u3, tm):
    b, s, _ = u3.shape
    per = tm // HIST_ROWS
    return pl.pallas_call(
        functools.partial(_pool_prompt_kernel, tm=tm),
        grid=(b, s // tm),
        in_specs=[pl.BlockSpec((None, tm, POOL_WIDTH), lambda bi, i: (bi, i, 0)),
                  pl.BlockSpec((None, HIST_ROWS, POOL_WIDTH),
                               lambda bi, i: (bi, jnp.maximum(i * per - 1, 0), 0))],
        out_specs=pl.BlockSpec((None, tm, POOL_WIDTH), lambda bi, i: (bi, i, 0)),
        out_shape=jax.ShapeDtypeStruct(u3.shape, F32),
        scratch_shapes=[pltpu.VMEM((HIST_ROWS + tm, POOL_WIDTH), F32)],
        compiler_params=_params(2),
        name="pool_prompt",
    )(u3, u3)


def _pool_sample_kernel(sp_ref, u_ref, o_ref, st_ref, *, past, n_new):
    def ext(r, sl):
        if r < POOL_HIST:
            return sp_ref[:, r * POOL_WIDTH + sl.start: r * POOL_WIDTH + sl.stop]
        r -= POOL_HIST
        return u_ref[:, r * POOL_WIDTH + sl.start: r * POOL_WIDTH + sl.stop]

    for t in range(n_new):
        for g, w in enumerate(POOL_WINDOWS):
            sl = slice(g * POOL_GROUP_WIDTH, (g + 1) * POOL_GROUP_WIDTH)
            cur = ext(POOL_HIST + t, sl)
            tot = cur
            for k in range(1, w):
                tot = tot + ext(POOL_HIST + t - k, sl)
            cnt = float(min(past + t + 1, w))
            o_ref[:, t * POOL_WIDTH + sl.start: t * POOL_WIDTH + sl.stop] = tot / cnt - cur
    keep = (POOL_HIST - n_new) * POOL_WIDTH
    st_ref[:, :keep] = sp_ref[:, n_new * POOL_WIDTH:]
    st_ref[:, keep:] = u_ref[...]


def _pool_sample(sp2, u2, past, n_new):
    return pl.pallas_call(
        functools.partial(_pool_sample_kernel, past=past, n_new=n_new),
        out_shape=[jax.ShapeDtypeStruct(u2.shape, F32), jax.ShapeDtypeStruct(sp2.shape, F32)],
        compiler_params=pltpu.CompilerParams(vmem_limit_bytes=VMEM_LIMIT),
        name="pool_sample",
    )(sp2, u2)


def _lambda(lp_ref, lam_init):
    lp = lp_ref[...]
    a = jnp.sum(lp[0:1] * lp[1:2], axis=-1, keepdims=True)
    b = jnp.sum(lp[2:3] * lp[3:4], axis=-1, keepdims=True)
    return jnp.exp(a) - jnp.exp(b) + lam_init


def _attn_kernel(lp_ref, q_ref, k_ref, v_ref, o_ref, qs_ref, s_ref, m_ref, acc_ref,
                 *, tq, tk, lam_init):
    qi = pl.program_id(2)
    rows = 2 * GROUP * tq
    lane = lax.broadcasted_iota(jnp.int32, (tq, LANES), 1)
    zero = jnp.zeros((tq, LANES), BF16)
    for g in range(GROUP):
        qg = q_ref[:, g * LANES:(g + 1) * LANES]
        qs_ref[(2 * g) * tq:(2 * g + 1) * tq, :] = jnp.where(lane < HEAD_DIM, qg, zero)
        qs_ref[(2 * g + 1) * tq:(2 * g + 2) * tq, :] = jnp.where(lane >= HEAD_DIM, qg, zero)
    m_ref[...] = jnp.full((rows, LANES), NEG, F32)
    acc_ref[...] = jnp.zeros((rows, 2 * V_DIM), F32)

    def scores(j, slot):
        start = pl.multiple_of(j * tk, tk)
        s_ref[slot] = lax.dot_general(qs_ref[...], k_ref[pl.ds(start, tk), :],
                                      (((1,), (1,)), ((), ())), preferred_element_type=F32)

    def accumulate(j, slot, masked):
        start = pl.multiple_of(j * tk, tk)
        s = s_ref[slot]
        if masked:
            r = lax.broadcasted_iota(jnp.int32, (rows, tk), 0)
            qpos = qi * tq + lax.rem(r, tq)
            kpos = start + lax.broadcasted_iota(jnp.int32, (rows, tk), 1)
            s = jnp.where(kpos <= qpos, s, NEG)
        m_prev = m_ref[...]
        m_next = jnp.maximum(m_prev, jnp.max(s, axis=1, keepdims=True))
        p = jnp.exp2(s - pltpu.repeat(m_next, tk // LANES, axis=1)).astype(BF16)
        alpha = jnp.exp2(m_prev - m_next)
        pv = jnp.dot(p, v_ref[pl.ds(start, tk), :], preferred_element_type=F32)
        acc_ref[...] = pltpu.repeat(alpha, 2, axis=1) * acc_ref[...] + pv
        m_ref[...] = m_next

    n = ((qi + 1) * tq + tk - 1) // tk
    n_pairs = (n - 1) // 2
    scores(0, 0)

    def pair_body(jj, c):
        j = 2 * jj
        scores(j + 1, 1)
        accumulate(j, 0, False)
        scores(j + 2, 0)
        accumulate(j + 1, 1, False)
        return c

    lax.fori_loop(0, n_pairs, pair_body, 0)
    odd_tail = n - 2 * n_pairs == 1

    @pl.when(odd_tail)
    def _():
        accumulate(n - 1, 0, True)

    @pl.when(jnp.logical_not(odd_tail))
    def _():
        scores(n - 1, 1)
        accumulate(n - 2, 0, False)
        accumulate(n - 1, 1, True)

    lam = _lambda(lp_ref, lam_init)
    acc = acc_ref[...]
    o = acc[:, :V_DIM] / acc[:, V_DIM:]
    for g in range(GROUP):
        o0 = o[(2 * g) * tq:(2 * g + 1) * tq]
        o1 = o[(2 * g + 1) * tq:(2 * g + 2) * tq]
        o_ref[:, g * V_DIM:(g + 1) * V_DIM] = o0 - lam * o1


def _attn_prompt(lp, q3, k3, v3, lam_init, tq, tk):
    b, s, _ = q3.shape
    rows = 2 * GROUP * tq
    qw = GROUP * 2 * HEAD_DIM
    once = pl.Buffered(1)
    return pl.pallas_call(
        functools.partial(_attn_kernel, tq=tq, tk=tk, lam_init=lam_init),
        grid=(b, N_KV_HEADS, s // tq),
        in_specs=[_const_spec(lp.shape),
                  pl.BlockSpec((None, tq, qw), lambda bi, h, i: (bi, i, h)),
                  pl.BlockSpec((None, s, LANES), lambda bi, h, i: (bi, 0, h), pipeline_mode=once),
                  pl.BlockSpec((None, s, 2 * V_DIM), lambda bi, h, i: (bi, 0, h),
                               pipeline_mode=once)],
        out_specs=pl.BlockSpec((None, tq, GROUP * V_DIM), lambda bi, h, i: (bi, i, h)),
        out_shape=jax.ShapeDtypeStruct((b, s, ATTN_WIDTH), F32),
        scratch_shapes=[pltpu.VMEM((rows, LANES), BF16), pltpu.VMEM((2, rows, tk), F32),
                        pltpu.VMEM((rows, LANES), F32), pltpu.VMEM((rows, 2 * V_DIM), F32)],
        compiler_params=_params(3),
        name="attn_prompt",
    )(lp, q3, k3, v3)


def _sattn_kernel(pt_ref, lp_ref, q_ref, kn_ref, vn_ref, *rest, n_pages_step, n_new, lam_init):
    del pt_ref
    k_refs = rest[:n_pages_step]
    v_refs = rest[n_pages_step:2 * n_pages_step]
    o_ref, qall_ref, new_ref, m_ref, l_ref, acc_ref = rest[2 * n_pages_step:]
    step_i = pl.program_id(1)
    half = GROUP * n_new
    hrows = 2 * half
    rows = N_KV_HEADS * hrows

    def update(s, v_of_head):
        m_prev = m_ref[...]
        m_next = jnp.maximum(m_prev, jnp.max(s, axis=1, keepdims=True))
        p = jnp.exp2(s - pltpu.repeat(m_next, s.shape[1] // LANES, axis=1))
        alpha = jnp.exp2(m_prev - m_next)
        l_ref[...] = alpha * l_ref[...] + jnp.sum(p, axis=1, keepdims=True)
        p = p.astype(BF16)
        for h in range(N_KV_HEADS):
            hs = slice(h * hrows, (h + 1) * hrows)
            pv = jnp.dot(p[hs], v_of_head(h), preferred_element_type=F32)
            acc_ref[hs, :] = alpha[hs] * acc_ref[hs, :] + pv
        m_ref[...] = m_next

    @pl.when(step_i == 0)
    def _():
        qall_ref[...] = jnp.zeros(qall_ref.shape, F32)
        lane = lax.broadcasted_iota(jnp.int32, (half, LANES), 1)
        zero = jnp.zeros((half, LANES), F32)
        for h in range(N_KV_HEADS):
            qh = q_ref[h]
            r0 = h * hrows
            qall_ref[r0:r0 + half, h * LANES:(h + 1) * LANES] = jnp.where(lane < HEAD_DIM, qh, zero)
            qall_ref[r0 + half:r0 + hrows, h * LANES:(h + 1) * LANES] = \
                jnp.where(lane >= HEAD_DIM, qh, zero)
        m_ref[...] = jnp.full(m_ref.shape, NEG, F32)
        l_ref[...] = jnp.zeros(l_ref.shape, F32)
        acc_ref[...] = jnp.zeros(acc_ref.shape, F32)
        new_ref[...] = jnp.zeros(new_ref.shape, F32)
        new_ref[0, 0:n_new, :] = kn_ref[...]
        new_ref[1, 0:n_new, :] = vn_ref[...]
        s = lax.dot_general(qall_ref[...].astype(BF16), new_ref[0].astype(BF16),
                            (((1,), (1,)), ((), ())), preferred_element_type=F32)
        t_row = lax.rem(lax.broadcasted_iota(jnp.int32, (rows, LANES), 0), n_new)
        t_key = lax.broadcasted_iota(jnp.int32, (rows, LANES), 1)
        s = jnp.where(t_key <= t_row, s, NEG)
        update(s, lambda h: new_ref[1, :, h * V_DIM:(h + 1) * V_DIM].astype(BF16))

    qall = qall_ref[...].astype(BF16)
    s = jnp.concatenate(
        [jnp.dot(qall, kr[...].astype(BF16), preferred_element_type=F32) for kr in k_refs], axis=1)

    def page_values(h):
        return jnp.concatenate(
            [vr[pl.ds(h, PAGE_SIZE, stride=N_KV_HEADS), :].astype(BF16) for vr in v_refs], axis=0)

    update(s, page_values)

    @pl.when(step_i == pl.num_programs(1) - 1)
    def _():
        lam = _lambda(lp_ref, lam_init)
        o = acc_ref[...] / l_ref[...]
        for h in range(N_KV_HEADS):
            r0 = h * hrows
            o_ref[h] = o[r0:r0 + half] - lam * o[r0 + half:r0 + hrows]


def _attn_sample(page_table, lp, q4, kn3, vn3, kt_pages, v_pages, lam_init, n_pages_step):
    b, n_pages = page_table.shape
    n_new = kn3.shape[1]
    rows = N_KV_HEADS * 2 * GROUP * n_new
    steps = n_pages // n_pages_step

    def page_spec(n, shape):
        return pl.BlockSpec((None,) + shape,
                            lambda bi, si, pt: (pt[bi, si * n_pages_step + n], 0, 0))

    grid_spec = pltpu.PrefetchScalarGridSpec(
        num_scalar_prefetch=1,
        grid=(b, steps),
        in_specs=[pl.BlockSpec(lp.shape, lambda bi, si, pt: (0, 0)),
                  pl.BlockSpec((None,) + q4.shape[1:], lambda bi, si, pt: (bi, 0, 0, 0)),
                  pl.BlockSpec((None, n_new, K_WIDTH), lambda bi, si, pt: (bi, 0, 0)),
                  pl.BlockSpec((None, n_new, V_WIDTH), lambda bi, si, pt: (bi, 0, 0))]
                 + [page_spec(n, (K_WIDTH, PAGE_SIZE)) for n in range(n_pages_step)]
                 + [page_spec(n, (PAGE_SIZE * N_KV_HEADS, V_DIM)) for n in range(n_pages_step)],
        out_specs=pl.BlockSpec((None,) + q4.shape[1:], lambda bi, si, pt: (bi, 0, 0, 0)),
        scratch_shapes=[pltpu.VMEM((rows, K_WIDTH), F32),
                        pltpu.VMEM((2, PAGE_SIZE, K_WIDTH), F32),
                        pltpu.VMEM((rows, LANES), F32), pltpu.VMEM((rows, LANES), F32),
                        pltpu.VMEM((rows, V_DIM), F32)])
    return pl.pallas_call(
        functools.partial(_sattn_kernel, n_pages_step=n_pages_step, n_new=n_new,
                          lam_init=lam_init),
        grid_spec=grid_spec,
        out_shape=jax.ShapeDtypeStruct(q4.shape, F32),
        compiler_params=_params(2),
        name="attn_sample",
    )(page_table, lp, q4, kn3, vn3, *([kt_pages] * n_pages_step), *([v_pages] * n_pages_step))


def _merge_kernel(x_ref, pooled_ref, attn_ref, gate_ref, wpg_ref, ps_ref, sg_ref,
                  wpp_ref, wap_ref, wo_ref, o_ref, *, lam_init):
    pooled = pooled_ref[...].astype(BF16)
    mixed = [jnp.dot(pooled[:, g * POOL_GROUP_WIDTH:(g + 1) * POOL_GROUP_WIDTH], wpg_ref[g],
                     preferred_element_type=F32) for g in range(N_POOL_GROUPS)]
    pool_out = jnp.concatenate(mixed, axis=1) * ps_ref[...]
    a_branch = jnp.dot(pool_out.astype(BF16), wpp_ref[...], preferred_element_type=F32)
    sg = sg_ref[...]
    heads = []
    for h in range(N_HEADS):
        oh = attn_ref[:, h * V_DIM:(h + 1) * V_DIM]
        heads.append((_rms(oh, sg) * (1.0 - lam_init)).astype(BF16))
    b_branch = jnp.dot(jnp.concatenate(heads, axis=1), wap_ref[...], preferred_element_type=F32)
    ga = gate_ref[:, :D_MODEL]
    gb = gate_ref[:, D_MODEL:]
    mix = jax.nn.sigmoid(ga) * a_branch + jax.nn.sigmoid(gb) * b_branch
    o_ref[...] = x_ref[...] + jnp.dot(mix.astype(BF16), wo_ref[...], preferred_element_type=F32)


def _merge(x2d, pooled, attn, gates, wpg, ps, sg, wpp, wap, wo, lam_init, tm):
    n = x2d.shape[0]
    row = lambda i: (i, 0)
    return pl.pallas_call(
        functools.partial(_merge_kernel, lam_init=lam_init),
        grid=(n // tm,),
        in_specs=[pl.BlockSpec((tm, D_MODEL), row), pl.BlockSpec((tm, POOL_WIDTH), row),
                  pl.BlockSpec((tm, ATTN_WIDTH), row), pl.BlockSpec((tm, 2 * D_MODEL), row),
                  _const_spec(wpg.shape), _const_spec(ps.shape), _const_spec(sg.shape),
                  _const_spec(wpp.shape), _const_spec(wap.shape), _const_spec(wo.shape)],
        out_specs=pl.BlockSpec((tm, D_MODEL), row),
        out_shape=jax.ShapeDtypeStruct((n, D_MODEL), F32),
        compiler_params=_params(1),
        name="merge",
    )(x2d, pooled, attn, gates, wpg, ps, sg, wpp, wap, wo)


def _ffn_kernel(x_ref, nf_ref, wg_ref, wu_ref, wd_ref, fn_ref, o_ref):
    x = x_ref[...]
    h = _rms(x, nf_ref[...]).astype(BF16)
    gate = jnp.dot(h, wg_ref[...], preferred_element_type=F32)
    up = jnp.dot(h, wu_ref[...], preferred_element_type=F32)
    act = (jax.nn.silu(gate) * up).astype(BF16)
    y = x + jnp.dot(act, wd_ref[...], preferred_element_type=F32)
    o_ref[...] = _rms(y, fn_ref[...])


def _ffn(x2d, nf, wg, wu, wd, fn, tm):
    n = x2d.shape[0]
    row = lambda i: (i, 0)
    return pl.pallas_call(
        _ffn_kernel,
        grid=(n // tm,),
        in_specs=[pl.BlockSpec((tm, D_MODEL), row), _const_spec(nf.shape), _const_spec(wg.shape),
                  _const_spec(wu.shape), _const_spec(wd.shape), _const_spec(fn.shape)],
        out_specs=pl.BlockSpec((tm, D_MODEL), row),
        out_shape=jax.ShapeDtypeStruct((n, D_MODEL), F32),
        compiler_params=_params(1),
        name="ffn",
    )(x2d, nf, wg, wu, wd, fn)


def kernel(x_prompt, x_sample, cache_k, cache_v, state_pool, page_table, norm_mix, w_in,
           w_pool_group, pool_scale, lambda_q1, lambda_k1, lambda_q2, lambda_k2, subln_g,
           w_pool_proj, w_attn_proj, w_out, norm_ffn, w_gate, w_up, w_down, final_norm):
    depth = w_in.shape[0]
    assert depth == 1, "single-layer step"
    batch, seq, _ = x_prompt.shape
    dec_batch, dec_seq, _ = x_sample.shape
    n_pages = page_table.shape[1]
    past = n_pages * PAGE_SIZE
    n_pool = cache_k.shape[1]
    l = 0
    lam_init = 0.8 - 0.6 * math.exp(-0.3 * l)

    tm = 512
    row2 = lambda a: a.reshape(1, -1)
    w_in_bf = w_in[l].astype(BF16)
    wpg = w_pool_group[l].astype(BF16)
    wpp = w_pool_proj[l].astype(BF16)
    wap = w_attn_proj[l].astype(BF16)
    wo = w_out[l].astype(BF16)
    wg = w_gate[l].astype(BF16)
    wu = w_up[l].astype(BF16)
    wd = w_down[l].astype(BF16)
    g_mix, ps, sg = row2(norm_mix[l]), row2(pool_scale[l]), row2(subln_g[l])
    nf, fn = row2(norm_ffn[l]), row2(final_norm)
    lp = jnp.stack([lambda_q1[l], lambda_k1[l], lambda_q2[l], lambda_k2[l]]).astype(F32)

    def tail(x2d, pooled, attn, gates, tile):
        x1 = _merge(x2d, pooled, attn, gates, wpg, ps, sg, wpp, wap, wo, lam_init, tile)
        return _ffn(x1, nf, wg, wu, wd, fn, tile)

    n_p = batch * seq
    tabs_p = _rope_tables(jnp.arange(seq, dtype=jnp.int32))
    u3, q, kt, kb, vf, vb, gates = _proj(x_prompt, g_mix, w_in_bf, *tabs_p, tm, True)
    pooled = _pool_prompt(u3, tm).reshape(n_p, POOL_WIDTH)
    attn = _attn_prompt(lp, q, kb, vb, lam_init, 256, 1024)
    y_prompt = tail(x_prompt.reshape(n_p, D_MODEL), pooled, attn.reshape(n_p, ATTN_WIDTH),
                    gates.reshape(n_p, 2 * D_MODEL), tm).reshape(x_prompt.shape)
    k_prompt = kt.reshape(1, batch, N_KV_HEADS, 2, HEAD_DIM, seq).transpose(0, 1, 5, 2, 3, 4)
    v_prompt = vf.reshape(1, batch, seq, N_KV_HEADS, V_DIM)
    pool_prompt = u3[:, seq - POOL_HIST:, :][None]

    n_s = dec_batch * dec_seq
    pos_s = past + jnp.arange(dec_seq, dtype=jnp.int32)
    tabs_s = [jnp.tile(t, (dec_batch, 1)) for t in _rope_tables(pos_s)]
    u, q, kf, _, vf, _, gates = _proj(x_sample.reshape(1, n_s, D_MODEL), g_mix, w_in_bf, *tabs_s,
                                      n_s, False)
    pooled, pool_state = _pool_sample(
        state_pool[l].reshape(dec_batch, POOL_HIST * POOL_WIDTH),
        u.reshape(dec_batch, dec_seq * POOL_WIDTH), past, dec_seq)
    q4 = q.reshape(dec_batch, dec_seq, N_KV_HEADS, GROUP, V_DIM).transpose(0, 2, 3, 1, 4)
    q4 = q4.reshape(dec_batch, N_KV_HEADS, GROUP * dec_seq, V_DIM).astype(F32)
    kt_pages = cache_k.transpose(0, 1, 3, 4, 5, 2).reshape(n_pool, K_WIDTH, PAGE_SIZE)
    v_pages = cache_v.reshape(n_pool, PAGE_SIZE * N_KV_HEADS, V_DIM)
    o4 = _attn_sample(page_table, lp, q4, kf.reshape(dec_batch, dec_seq, K_WIDTH),
                      vf.reshape(dec_batch, dec_seq, V_WIDTH), kt_pages, v_pages, lam_init, 16)
    attn = o4.reshape(dec_batch, N_KV_HEADS, GROUP, dec_seq, V_DIM).transpose(0, 3, 1, 2, 4)
    y_sample = tail(x_sample.reshape(n_s, D_MODEL), pooled.reshape(n_s, POOL_WIDTH),
                    attn.reshape(n_s, ATTN_WIDTH), gates.reshape(n_s, 2 * D_MODEL),
                    n_s).reshape(x_sample.shape)
    k_sample = kf.reshape(1, dec_batch, dec_seq, N_KV_HEADS, 2, HEAD_DIM)
    v_sample = vf.reshape(1, dec_batch, dec_seq, N_KV_HEADS, V_DIM)
    pool_sample = pool_state.reshape(1, dec_batch, POOL_HIST, POOL_WIDTH)

    return (y_prompt, y_sample, k_prompt, v_prompt, pool_prompt,
            k_sample, v_sample, pool_sample)
```

```python
import functools
import math

import jax
import jax.numpy as jnp
from jax import lax
from jax.experimental import pallas as pl
from jax.experimental.pallas import tpu as pltpu

D_MODEL = 1024
PAGE_SIZE = 128
POOL_WINDOWS = (2, 4, 8, 16)
N_POOL_GROUPS = len(POOL_WINDOWS)
POOL_GROUP_WIDTH = D_MODEL // 8
POOL_WIDTH = N_POOL_GROUPS * POOL_GROUP_WIDTH
POOL_HIST = max(POOL_WINDOWS) - 1
N_HEADS = 8
N_KV_HEADS = 4
GROUP = N_HEADS // N_KV_HEADS
HEAD_DIM = D_MODEL // N_HEADS // 2
V_DIM = 2 * HEAD_DIM
ROT_DIM = HEAD_DIM // 4
ROPE_THETA = 500000.0
Q_WIDTH = N_HEADS * 2 * HEAD_DIM
K_WIDTH = N_KV_HEADS * 2 * HEAD_DIM
V_WIDTH = N_KV_HEADS * V_DIM
ATTN_WIDTH = N_HEADS * V_DIM
D_FF = ((8 * D_MODEL + 3 * 256 - 1) // (3 * 256)) * 256
EPS = 1e-6
NEG = -1e30
Q_SCALE = HEAD_DIM ** -0.5 * math.log2(math.e)

LANES = 128
HIST_ROWS = 16
VMEM_LIMIT = 56 * 1024 * 1024

F32 = jnp.float32
BF16 = jnp.bfloat16

O_Q = POOL_WIDTH
O_K = O_Q + Q_WIDTH
O_V = O_K + K_WIDTH
O_G = O_V + V_WIDTH
IN_WIDTH = O_G + 2 * D_MODEL


def _const_spec(shape):
    zeros = (0,) * len(shape)
    return pl.BlockSpec(shape, lambda *_: zeros, pipeline_mode=pl.Buffered(1))


def _params(n_grid):
    return pltpu.CompilerParams(
        dimension_semantics=("arbitrary",) * n_grid, vmem_limit_bytes=VMEM_LIMIT)


def _rms(x, g):
    return x * lax.rsqrt(jnp.mean(x * x, axis=-1, keepdims=True) + EPS) * g


def _proj_kernel(x_ref, g_ref, w_ref, cos_ref, sa_ref, sb_ref,
                 u_ref, q_ref, kf_ref, kb_ref, vf_ref, vb_ref, gate_ref, *, k_transposed):
    tm = x_ref.shape[0]
    xn = _rms(x_ref[...], g_ref[...]).astype(BF16)

    def seg(lo, width):
        return jnp.dot(xn, w_ref[:, lo:lo + width], preferred_element_type=F32)

    cos, sa, sb = cos_ref[...], sa_ref[...], sb_ref[...]

    def rope(z):
        return z * cos + pltpu.roll(z, LANES - ROT_DIM // 2, 1) * sa + pltpu.roll(z, ROT_DIM // 2, 1) * sb

    u_ref[...] = seg(0, POOL_WIDTH)
    zq = seg(O_Q, Q_WIDTH)
    for i in range(Q_WIDTH // LANES):
        sl = slice(i * LANES, (i + 1) * LANES)
        q_ref[:, sl] = (rope(zq[:, sl]) * Q_SCALE).astype(BF16)
    zk = seg(O_K, K_WIDTH)
    for i in range(K_WIDTH // LANES):
        sl = slice(i * LANES, (i + 1) * LANES)
        kr = rope(zk[:, sl])
        if k_transposed:
            kf_ref[sl, :] = kr.T
        else:
            kf_ref[:, sl] = kr
        kb_ref[:, sl] = kr.astype(BF16)
    zv = seg(O_V, V_WIDTH)
    for h in range(N_KV_HEADS):
        vf_ref[pl.ds(h, tm, stride=N_KV_HEADS), :] = zv[:, h * V_DIM:(h + 1) * V_DIM]
        vb_ref[:, 2 * h * V_DIM:(2 * h + 1) * V_DIM] = zv[:, h * V_DIM:(h + 1) * V_DIM].astype(BF16)
        vb_ref[:, (2 * h + 1) * V_DIM:(2 * h + 2) * V_DIM] = jnp.ones((tm, V_DIM), BF16)
    gate_ref[...] = seg(O_G, 2 * D_MODEL)


def _proj(x3d, g, w_bf, cos, sa, sb, tm, k_transposed):
    b, t, _ = x3d.shape
    row = lambda bi, i: (bi, i, 0)
    tab = lambda bi, i: (i, 0)

    def rows(w, dt):
        return (pl.BlockSpec((None, tm, w), row), jax.ShapeDtypeStruct((b, t, w), dt))

    if k_transposed:
        k_out = (pl.BlockSpec((None, K_WIDTH, tm), lambda bi, i: (bi, 0, i)),
                 jax.ShapeDtypeStruct((b, K_WIDTH, t), F32))
    else:
        k_out = rows(K_WIDTH, F32)
    v_out = (pl.BlockSpec((None, tm * N_KV_HEADS, V_DIM), row),
             jax.ShapeDtypeStruct((b, t * N_KV_HEADS, V_DIM), F32))
    outs = [rows(POOL_WIDTH, F32), rows(Q_WIDTH, BF16), k_out, rows(K_WIDTH, BF16),
            v_out, rows(2 * V_WIDTH, BF16), rows(2 * D_MODEL, F32)]
    return pl.pallas_call(
        functools.partial(_proj_kernel, k_transposed=k_transposed),
        grid=(b, t // tm),
        in_specs=[pl.BlockSpec((None, tm, D_MODEL), row), _const_spec((1, D_MODEL)),
                  _const_spec((D_MODEL, IN_WIDTH)),
                  pl.BlockSpec((tm, LANES), tab), pl.BlockSpec((tm, LANES), tab),
                  pl.BlockSpec((tm, LANES), tab)],
        out_specs=[o[0] for o in outs],
        out_shape=[o[1] for o in outs],
        compiler_params=_params(2),
        name="proj",
    )(x3d, g, w_bf, cos, sa, sb)


def _rope_tables(pos):
    half = ROT_DIM // 2
    j = jnp.arange(LANES, dtype=jnp.int32) % HEAD_DIM
    inv = ROPE_THETA ** (-(j % half).astype(F32) * 2.0 / ROT_DIM)
    ang = pos.astype(F32)[:, None] * inv[None, :]
    c, s = jnp.cos(ang), jnp.sin(ang)
    cos = jnp.where(j < ROT_DIM, c, 1.0)
    sa = jnp.where(j < half, -s, 0.0)
    sb = jnp.where((j >= half) & (j < ROT_DIM), s, 0.0)
    return cos, sa, sb


def _pool_prompt_kernel(u_ref, hist_ref, o_ref, ext_ref, *, tm):
    i = pl.program_id(1)
    u = u_ref[...]
    ext_ref[0:HIST_ROWS, :] = jnp.where(i == 0, 0.0, hist_ref[...])
    ext_ref[HIST_ROWS:, :] = u
    pos = i * tm + lax.broadcasted_iota(jnp.int32, (tm, 1), 0)
    for g, w in enumerate(POOL_WINDOWS):
        sl = slice(g * POOL_GROUP_WIDTH, (g + 1) * POOL_GROUP_WIDTH)
        tot = u[:, sl]
        for k in range(1, w):
            tot = tot + ext_ref[pl.ds(HIST_ROWS - k, tm), sl]
        cnt = jnp.minimum(pos + 1, w).astype(F32)
        o_ref[:, sl] = tot / cnt - u[:, sl]


def _pool_prompt(u3, tm):
    b, s, _ = u3.shape
    per = tm // HIST_ROWS
    return pl.pallas_call(
        functools.partial(_pool_prompt_kernel, tm=tm),
        grid=(b, s // tm),
        in_specs=[pl.BlockSpec((None, tm, POOL_WIDTH), lambda bi, i: (bi, i, 0)),
                  pl.BlockSpec((None, HIST_ROWS, POOL_WIDTH),
                               lambda bi, i: (bi, jnp.maximum(i * per - 1, 0), 0))],
        out_specs=pl.BlockSpec((None, tm, POOL_WIDTH), lambda bi, i: (bi, i, 0)),
        out_shape=jax.ShapeDtypeStruct(u3.shape, F32),
        scratch_shapes=[pltpu.VMEM((HIST_ROWS + tm, POOL_WIDTH), F32)],
        compiler_params=_params(2),
        name="pool_prompt",
    )(u3, u3)


def _pool_sample_kernel(sp_ref, u_ref, o_ref, st_ref, *, past, n_new):
    def ext(r, sl):
        if r < POOL_HIST:
            return sp_ref[:, r * POOL_WIDTH + sl.start: r * POOL_WIDTH + sl.stop]
        r -= POOL_HIST
        return u_ref[:, r * POOL_WIDTH + sl.start: r * POOL_WIDTH + sl.stop]

    for t in range(n_new):
        for g, w in enumerate(POOL_WINDOWS):
            sl = slice(g * POOL_GROUP_WIDTH, (g + 1) * POOL_GROUP_WIDTH)
            cur = ext(POOL_HIST + t, sl)
            tot = cur
            for k in range(1, w):
                tot = tot + ext(POOL_HIST + t - k, sl)
            cnt = float(min(past + t + 1, w))
            o_ref[:, t * POOL_WIDTH + sl.start: t * POOL_WIDTH + sl.stop] = tot / cnt - cur
    keep = (POOL_HIST - n_new) * POOL_WIDTH
    st_ref[:, :keep] = sp_ref[:, n_new * POOL_WIDTH:]
    st_ref[:, keep:] = u_ref[...]


def _pool_sample(sp2, u2, past, n_new):
    return pl.pallas_call(
        functools.partial(_pool_sample_kernel, past=past, n_new=n_new),
        out_shape=[jax.ShapeDtypeStruct(u2.shape, F32), jax.ShapeDtypeStruct(sp2.shape, F32)],
        compiler_params=pltpu.CompilerParams(vmem_limit_bytes=VMEM_LIMIT),
        name="pool_sample",
    )(sp2, u2)


def _lambda(lp_ref, lam_init):
    lp = lp_ref[...]
    a = jnp.sum(lp[0:1] * lp[1:2], axis=-1, keepdims=True)
    b = jnp.sum(lp[2:3] * lp[3:4], axis=-1, keepdims=True)
    return jnp.exp(a) - jnp.exp(b) + lam_init


def _sample_tile_update(s, v_of_head, m_ref, l_ref, acc_ref, hrows):
    m_prev = m_ref[...]
    m_next = jnp.maximum(m_prev, jnp.max(s, axis=1, keepdims=True))
    p = jnp.exp2(s - pltpu.repeat(m_next, s.shape[1] // LANES, axis=1))
    alpha = jnp.exp2(m_prev - m_next)
    l_ref[...] = alpha * l_ref[...] + jnp.sum(p, axis=1, keepdims=True)
    p = p.astype(BF16)
    for h in range(N_KV_HEADS):
        hs = slice(h * hrows, (h + 1) * hrows)
        pv = jnp.dot(p[hs], v_of_head(h), preferred_element_type=F32)
        acc_ref[hs, :] = alpha[hs] * acc_ref[hs, :] + pv
    m_ref[...] = m_next


def _attn_kernel(pt_ref, lp_ref, q_ref, k_ref, v_ref, sq_ref, kn_ref, vn_ref, *rest,
                 tq, tk, n_pages_step, steps_per_seq, n_new, lam_init):
    del pt_ref
    kp_refs = rest[:n_pages_step]
    vp_refs = rest[n_pages_step:2 * n_pages_step]
    (o_ref, so_ref, qs_ref, s_ref, m_ref, acc_ref,
     qall_ref, new_ref, sm_ref, sl_ref, sacc_ref) = rest[2 * n_pages_step:]
    qi = pl.program_id(2)
    flat = (pl.program_id(0) * N_KV_HEADS + pl.program_id(1)) * pl.num_programs(2) + qi
    grp = lax.rem(flat, steps_per_seq)
    half = GROUP * n_new
    hrows = 2 * half
    srows = N_KV_HEADS * hrows

    @pl.when(grp == 0)
    def _():
        qall_ref[...] = jnp.zeros(qall_ref.shape, F32)
        lane = lax.broadcasted_iota(jnp.int32, (half, LANES), 1)
        zero = jnp.zeros((half, LANES), F32)
        for h in range(N_KV_HEADS):
            qh = sq_ref[h]
            r0 = h * hrows
            qall_ref[r0:r0 + half, h * LANES:(h + 1) * LANES] = jnp.where(lane < HEAD_DIM, qh, zero)
            qall_ref[r0 + half:r0 + hrows, h * LANES:(h + 1) * LANES] = \
                jnp.where(lane >= HEAD_DIM, qh, zero)
        sm_ref[...] = jnp.full(sm_ref.shape, NEG, F32)
        sl_ref[...] = jnp.zeros(sl_ref.shape, F32)
        sacc_ref[...] = jnp.zeros(sacc_ref.shape, F32)
        new_ref[...] = jnp.zeros(new_ref.shape, F32)
        new_ref[0, 0:n_new, :] = kn_ref[...]
        new_ref[1, 0:n_new, :] = vn_ref[...]
        s = lax.dot_general(qall_ref[...].astype(BF16), new_ref[0].astype(BF16),
                            (((1,), (1,)), ((), ())), preferred_element_type=F32)
        t_row = lax.rem(lax.broadcasted_iota(jnp.int32, (srows, LANES), 0), n_new)
        t_key = lax.broadcasted_iota(jnp.int32, (srows, LANES), 1)
        s = jnp.where(t_key <= t_row, s, NEG)
        _sample_tile_update(s, lambda h: new_ref[1, :, h * V_DIM:(h + 1) * V_DIM].astype(BF16),
                            sm_ref, sl_ref, sacc_ref, hrows)

    rows = 2 * GROUP * tq
    lane = lax.broadcasted_iota(jnp.int32, (tq, LANES), 1)
    zero = jnp.zeros((tq, LANES), BF16)
    for g in range(GROUP):
        qg = q_ref[:, g * LANES:(g + 1) * LANES]
        qs_ref[(2 * g) * tq:(2 * g + 1) * tq, :] = jnp.where(lane < HEAD_DIM, qg, zero)
        qs_ref[(2 * g + 1) * tq:(2 * g + 2) * tq, :] = jnp.where(lane >= HEAD_DIM, qg, zero)
    m_ref[...] = jnp.full((rows, LANES), NEG, F32)
    acc_ref[...] = jnp.zeros((rows, 2 * V_DIM), F32)

    def scores(j, slot):
        start = pl.multiple_of(j * tk, tk)
        s_ref[slot] = lax.dot_general(qs_ref[...], k_ref[pl.ds(start, tk), :],
                                      (((1,), (1,)), ((), ())), preferred_element_type=F32)

    def accumulate(j, slot, masked):
        start = pl.multiple_of(j * tk, tk)
        s = s_ref[slot]
        if masked:
            r = lax.broadcasted_iota(jnp.int32, (rows, tk), 0)
            qpos = qi * tq + lax.rem(r, tq)
            kpos = start + lax.broadcasted_iota(jnp.int32, (rows, tk), 1)
            s = jnp.where(kpos <= qpos, s, NEG)
        m_prev = m_ref[...]
        m_next = jnp.maximum(m_prev, jnp.max(s, axis=1, keepdims=True))
        p = jnp.exp2(s - pltpu.repeat(m_next, tk // LANES, axis=1)).astype(BF16)
        alpha = jnp.exp2(m_prev - m_next)
        pv = jnp.dot(p, v_ref[pl.ds(start, tk), :], preferred_element_type=F32)
        acc_ref[...] = pltpu.repeat(alpha, 2, axis=1) * acc_ref[...] + pv
        m_ref[...] = m_next

    n = ((qi + 1) * tq + tk - 1) // tk
    n_pairs = (n - 1) // 2
    scores(0, 0)

    qall = qall_ref[...].astype(BF16)
    s_pages = jnp.concatenate(
        [jnp.dot(qall, kr[...].astype(BF16), preferred_element_type=F32) for kr in kp_refs], axis=1)

    def page_values(h):
        return jnp.concatenate(
            [vr[pl.ds(h, PAGE_SIZE, stride=N_KV_HEADS), :].astype(BF16) for vr in vp_refs], axis=0)

    _sample_tile_update(s_pages, page_values, sm_ref, sl_ref, sacc_ref, hrows)

    @pl.when(grp == steps_per_seq - 1)
    def _():
        lam_s = _lambda(lp_ref, lam_init)
        o_s = sacc_ref[...] / sl_ref[...]
        for h in range(N_KV_HEADS):
            r0 = h * hrows
            so_ref[h] = o_s[r0:r0 + half] - lam_s * o_s[r0 + half:r0 + hrows]

    def pair_body(jj, c):
        j = 2 * jj
        scores(j + 1, 1)
        accumulate(j, 0, False)
        scores(j + 2, 0)
        accumulate(j + 1, 1, False)
        return c

    lax.fori_loop(0, n_pairs, pair_body, 0)
    odd_tail = n - 2 * n_pairs == 1

    @pl.when(odd_tail)
    def _():
        accumulate(n - 1, 0, True)

    @pl.when(jnp.logical_not(odd_tail))
    def _():
        scores(n - 1, 1)
        accumulate(n - 2, 0, False)
        accumulate(n - 1, 1, True)

    lam = _lambda(lp_ref, lam_init)
    acc = acc_ref[...]
    o = acc[:, :V_DIM] / acc[:, V_DIM:]
    for g in range(GROUP):
        o0 = o[(2 * g) * tq:(2 * g + 1) * tq]
        o1 = o[(2 * g + 1) * tq:(2 * g + 2) * tq]
        o_ref[:, g * V_DIM:(g + 1) * V_DIM] = o0 - lam * o1


def _attention(page_table, lp, q3, k3, v3, sq4, kn3, vn3, kt_pages, v_pages, lam_init, tq, tk):
    b, s, _ = q3.shape
    n_seq, n_pages = page_table.shape
    n_new = kn3.shape[1]
    nq = s // tq
    n_steps = b * N_KV_HEADS * nq
    steps_per_seq, rem = divmod(n_steps, n_seq)
    assert rem == 0 and n_pages % steps_per_seq == 0, (n_steps, n_seq, n_pages)
    n_pages_step = n_pages // steps_per_seq
    rows = 2 * GROUP * tq
    srows = N_KV_HEADS * 2 * GROUP * n_new
    qw = GROUP * 2 * HEAD_DIM
    once = pl.Buffered(1)

    def seq_of(bi, h, i):
        return ((bi * N_KV_HEADS + h) * nq + i) // steps_per_seq

    step_pages = page_table.reshape(n_steps * n_pages_step)

    def page_spec(n, shape):
        return pl.BlockSpec(
            (None,) + shape,
            lambda bi, h, i, pt: (pt[((bi * N_KV_HEADS + h) * nq + i) * n_pages_step + n], 0, 0))

    sample_block = lambda bi, h, i, pt: (seq_of(bi, h, i), 0, 0, 0)
    new_block = lambda bi, h, i, pt: (seq_of(bi, h, i), 0, 0)
    grid_spec = pltpu.PrefetchScalarGridSpec(
        num_scalar_prefetch=1,
        grid=(b, N_KV_HEADS, nq),
        in_specs=[pl.BlockSpec(lp.shape, lambda bi, h, i, pt: (0, 0)),
                  pl.BlockSpec((None, tq, qw), lambda bi, h, i, pt: (bi, i, h)),
                  pl.BlockSpec((None, s, LANES), lambda bi, h, i, pt: (bi, 0, h),
                               pipeline_mode=once),
                  pl.BlockSpec((None, s, 2 * V_DIM), lambda bi, h, i, pt: (bi, 0, h),
                               pipeline_mode=once),
                  pl.BlockSpec((None,) + sq4.shape[1:], sample_block),
                  pl.BlockSpec((None, n_new, K_WIDTH), new_block),
                  pl.BlockSpec((None, n_new, V_WIDTH), new_block)]
                 + [page_spec(n, (K_WIDTH, PAGE_SIZE)) for n in range(n_pages_step)]
                 + [page_spec(n, (PAGE_SIZE * N_KV_HEADS, V_DIM)) for n in range(n_pages_step)],
        out_specs=[pl.BlockSpec((None, tq, GROUP * V_DIM), lambda bi, h, i, pt: (bi, i, h)),
                   pl.BlockSpec((None,) + sq4.shape[1:], sample_block)],
        scratch_shapes=[pltpu.VMEM((rows, LANES), BF16), pltpu.VMEM((2, rows, tk), F32),
                        pltpu.VMEM((rows, LANES), F32), pltpu.VMEM((rows, 2 * V_DIM), F32),
                        pltpu.VMEM((srows, K_WIDTH), F32),
                        pltpu.VMEM((2, PAGE_SIZE, K_WIDTH), F32),
                        pltpu.VMEM((srows, LANES), F32), pltpu.VMEM((srows, LANES), F32),
                        pltpu.VMEM((srows, V_DIM), F32)])
    return pl.pallas_call(
        functools.partial(_attn_kernel, tq=tq, tk=tk, n_pages_step=n_pages_step,
                          steps_per_seq=steps_per_seq, n_new=n_new, lam_init=lam_init),
        grid_spec=grid_spec,
        out_shape=[jax.ShapeDtypeStruct((b, s, ATTN_WIDTH), F32),
                   jax.ShapeDtypeStruct(sq4.shape, F32)],
        compiler_params=_params(3),
        name="attention",
    )(step_pages, lp, q3, k3, v3, sq4, kn3, vn3,
      *([kt_pages] * n_pages_step), *([v_pages] * n_pages_step))


def _merge_kernel(x_ref, pooled_ref, attn_ref, gate_ref, wpg_ref, ps_ref, sg_ref,
                  wpp_ref, wap_ref, wo_ref, o_ref, *, lam_init):
    pooled = pooled_ref[...].astype(BF16)
    mixed = [jnp.dot(pooled[:, g * POOL_GROUP_WIDTH:(g + 1) * POOL_GROUP_WIDTH], wpg_ref[g],
                     preferred_element_type=F32) for g in range(N_POOL_GROUPS)]
    pool_out = jnp.concatenate(mixed, axis=1) * ps_ref[...]
    a_branch = jnp.dot(pool_out.astype(BF16), wpp_ref[...], preferred_element_type=F32)
    sg = sg_ref[...]
    heads = []
    for h in range(N_HEADS):
        oh = attn_ref[:, h * V_DIM:(h + 1) * V_DIM]
        heads.append((_rms(oh, sg) * (1.0 - lam_init)).astype(BF16))
    b_branch = jnp.dot(jnp.concatenate(heads, axis=1), wap_ref[...], preferred_element_type=F32)
    ga = gate_ref[:, :D_MODEL]
    gb = gate_ref[:, D_MODEL:]
    mix = jax.nn.sigmoid(ga) * a_branch + jax.nn.sigmoid(gb) * b_branch
    o_ref[...] = x_ref[...] + jnp.dot(mix.astype(BF16), wo_ref[...], preferred_element_type=F32)


def _merge(x2d, pooled, attn, gates, wpg, ps, sg, wpp, wap, wo, lam_init, tm):
    n = x2d.shape[0]
    row = lambda i: (i, 0)
    return pl.pallas_call(
        functools.partial(_merge_kernel, lam_init=lam_init),
        grid=(n // tm,),
        in_specs=[pl.BlockSpec((tm, D_MODEL), row), pl.BlockSpec((tm, POOL_WIDTH), row),
                  pl.BlockSpec((tm, ATTN_WIDTH), row), pl.BlockSpec((tm, 2 * D_MODEL), row),
                  _const_spec(wpg.shape), _const_spec(ps.shape), _const_spec(sg.shape),
                  _const_spec(wpp.shape), _const_spec(wap.shape), _const_spec(wo.shape)],
        out_specs=pl.BlockSpec((tm, D_MODEL), row),
        out_shape=jax.ShapeDtypeStruct((n, D_MODEL), F32),
        compiler_params=_params(1),
        name="merge",
    )(x2d, pooled, attn, gates, wpg, ps, sg, wpp, wap, wo)


def _ffn_kernel(x_ref, nf_ref, wg_ref, wu_ref, wd_ref, fn_ref, o_ref):
    x = x_ref[...]
    h = _rms(x, nf_ref[...]).astype(BF16)
    gate = jnp.dot(h, wg_ref[...], preferred_element_type=F32)
    up = jnp.dot(h, wu_ref[...], preferred_element_type=F32)
    act = (jax.nn.silu(gate) * up).astype(BF16)
    y = x + jnp.dot(act, wd_ref[...], preferred_element_type=F32)
    o_ref[...] = _rms(y, fn_ref[...])


def _ffn(x2d, nf, wg, wu, wd, fn, tm):
    n = x2d.shape[0]
    row = lambda i: (i, 0)
    return pl.pallas_call(
        _ffn_kernel,
        grid=(n // tm,),
        in_specs=[pl.BlockSpec((tm, D_MODEL), row), _const_spec(nf.shape), _const_spec(wg.shape),
                  _const_spec(wu.shape), _const_spec(wd.shape), _const_spec(fn.shape)],
        out_specs=pl.BlockSpec((tm, D_MODEL), row),
        out_shape=jax.ShapeDtypeStruct((n, D_MODEL), F32),
        compiler_params=_params(1),
        name="ffn",
    )(x2d, nf, wg, wu, wd, fn)


def kernel(x_prompt, x_sample, cache_k, cache_v, state_pool, page_table, norm_mix, w_in,
           w_pool_group, pool_scale, lambda_q1, lambda_k1, lambda_q2, lambda_k2, subln_g,
           w_pool_proj, w_attn_proj, w_out, norm_ffn, w_gate, w_up, w_down, final_norm):
    depth = w_in.shape[0]
    assert depth == 1, "single-layer step"
    batch, seq, _ = x_prompt.shape
    dec_batch, dec_seq, _ = x_sample.shape
    n_pages = page_table.shape[1]
    past = n_pages * PAGE_SIZE
    n_pool = cache_k.shape[1]
    l = 0
    lam_init = 0.8 - 0.6 * math.exp(-0.3 * l)

    tm = 512
    row2 = lambda a: a.reshape(1, -1)
    w_in_bf = w_in[l].astype(BF16)
    wpg = w_pool_group[l].astype(BF16)
    wpp = w_pool_proj[l].astype(BF16)
    wap = w_attn_proj[l].astype(BF16)
    wo = w_out[l].astype(BF16)
    wg = w_gate[l].astype(BF16)
    wu = w_up[l].astype(BF16)
    wd = w_down[l].astype(BF16)
    g_mix, ps, sg = row2(norm_mix[l]), row2(pool_scale[l]), row2(subln_g[l])
    nf, fn = row2(norm_ffn[l]), row2(final_norm)
    lp = jnp.stack([lambda_q1[l], lambda_k1[l], lambda_q2[l], lambda_k2[l]]).astype(F32)

    def tail(x2d, pooled, attn, gates, tile):
        x1 = _merge(x2d, pooled, attn, gates, wpg, ps, sg, wpp, wap, wo, lam_init, tile)
        return _ffn(x1, nf, wg, wu, wd, fn, tile)

    n_p = batch * seq
    n_s = dec_batch * dec_seq
    tabs_p = _rope_tables(jnp.arange(seq, dtype=jnp.int32))
    tabs_s = [jnp.tile(t, (dec_batch, 1))
              for t in _rope_tables(past + jnp.arange(dec_seq, dtype=jnp.int32))]
    u3, q_p, kt, kb, vf_p, vb, gates_p = _proj(x_prompt, g_mix, w_in_bf, *tabs_p, tm, True)
    u_s, q_s, kf_s, _, vf_s, _, gates_s = _proj(x_sample.reshape(1, n_s, D_MODEL), g_mix, w_in_bf,
                                                *tabs_s, n_s, False)

    q4 = q_s.reshape(dec_batch, dec_seq, N_KV_HEADS, GROUP, V_DIM).transpose(0, 2, 3, 1, 4)
    q4 = q4.reshape(dec_batch, N_KV_HEADS, GROUP * dec_seq, V_DIM).astype(F32)
    kt_pages = cache_k.transpose(0, 1, 3, 4, 5, 2).reshape(n_pool, K_WIDTH, PAGE_SIZE)
    v_pages = cache_v.reshape(n_pool, PAGE_SIZE * N_KV_HEADS, V_DIM)
    attn_p, o4 = _attention(page_table, lp, q_p, kb, vb, q4,
                            kf_s.reshape(dec_batch, dec_seq, K_WIDTH),
                            vf_s.reshape(dec_batch, dec_seq, V_WIDTH),
                            kt_pages, v_pages, lam_init, 256, 1024)
    attn_s = o4.reshape(dec_batch, N_KV_HEADS, GROUP, dec_seq, V_DIM).transpose(0, 3, 1, 2, 4)

    pooled_p = _pool_prompt(u3, tm).reshape(n_p, POOL_WIDTH)
    y_prompt = tail(x_prompt.reshape(n_p, D_MODEL), pooled_p, attn_p.reshape(n_p, ATTN_WIDTH),
                    gates_p.reshape(n_p, 2 * D_MODEL), tm).reshape(x_prompt.shape)
    k_prompt = kt.reshape(1, batch, N_KV_HEADS, 2, HEAD_DIM, seq).transpose(0, 1, 5, 2, 3, 4)
    v_prompt = vf_p.reshape(1, batch, seq, N_KV_HEADS, V_DIM)
    pool_prompt = u3[:, seq - POOL_HIST:, :][None]

    pooled_s, pool_state = _pool_sample(
        state_pool[l].reshape(dec_batch, POOL_HIST * POOL_WIDTH),
        u_s.reshape(dec_batch, dec_seq * POOL_WIDTH), past, dec_seq)
    y_sample = tail(x_sample.reshape(n_s, D_MODEL), pooled_s.reshape(n_s, POOL_WIDTH),
                    attn_s.reshape(n_s, ATTN_WIDTH), gates_s.reshape(n_s, 2 * D_MODEL),
                    n_s).reshape(x_sample.shape)
    k_sample = kf_s.reshape(1, dec_batch, dec_seq, N_KV_HEADS, 2, HEAD_DIM)
    v_sample = vf_s.reshape(1, dec_batch, dec_seq, N_KV_HEADS, V_DIM)
    pool_sample = pool_state.reshape(1, dec_batch, POOL_HIST, POOL_WIDTH)

    return (y_prompt, y_sample, k_prompt, v_prompt, pool_prompt,
            k_sample, v_sample, pool_sample)
```

```python
import functools
import math

import jax
import jax.numpy as jnp
from jax import lax
from jax.experimental import pallas as pl
from jax.experimental.pallas import tpu as pltpu

D_MODEL = 1024
PAGE_SIZE = 128
POOL_WINDOWS = (2, 4, 8, 16)
N_POOL_GROUPS = len(POOL_WINDOWS)
POOL_GROUP_WIDTH = D_MODEL // 8
POOL_WIDTH = N_POOL_GROUPS * POOL_GROUP_WIDTH
POOL_HIST = max(POOL_WINDOWS) - 1
N_HEADS = 8
N_KV_HEADS = 4
GROUP = N_HEADS // N_KV_HEADS
HEAD_DIM = D_MODEL // N_HEADS // 2
V_DIM = 2 * HEAD_DIM
ROT_DIM = HEAD_DIM // 4
ROPE_THETA = 500000.0
Q_WIDTH = N_HEADS * 2 * HEAD_DIM
K_WIDTH = N_KV_HEADS * 2 * HEAD_DIM
V_WIDTH = N_KV_HEADS * V_DIM
ATTN_WIDTH = N_HEADS * V_DIM
D_FF = ((8 * D_MODEL + 3 * 256 - 1) // (3 * 256)) * 256
EPS = 1e-6
NEG = -1e30
Q_SCALE = HEAD_DIM ** -0.5 * math.log2(math.e)

LANES = 128
HIST_ROWS = 16
VMEM_LIMIT = 56 * 1024 * 1024

F32 = jnp.float32
BF16 = jnp.bfloat16

O_Q = POOL_WIDTH
O_K = O_Q + Q_WIDTH
O_V = O_K + K_WIDTH
O_G = O_V + V_WIDTH
IN_WIDTH = O_G + 2 * D_MODEL


def _const_spec(shape):
    zeros = (0,) * len(shape)
    return pl.BlockSpec(shape, lambda *_: zeros, pipeline_mode=pl.Buffered(1))


def _params(n_grid):
    return pltpu.CompilerParams(
        dimension_semantics=("arbitrary",) * n_grid, vmem_limit_bytes=VMEM_LIMIT)


def _lane_tile(x, n):
    return x if n == 1 else jnp.concatenate([x] * n, axis=1)


def _rms(x, g):
    return x * lax.rsqrt(jnp.mean(x * x, axis=-1, keepdims=True) + EPS) * g


def _proj_kernel(x_ref, g_ref, w_ref, cos_ref, sa_ref, sb_ref,
                 u_ref, q_ref, kf_ref, kb_ref, vf_ref, vb_ref, gate_ref, *, k_transposed):
    tm = x_ref.shape[0]
    xn = _rms(x_ref[...], g_ref[...]).astype(BF16)

    def seg(lo, width):
        return jnp.dot(xn, w_ref[:, lo:lo + width], preferred_element_type=F32)

    cos, sa, sb = cos_ref[...], sa_ref[...], sb_ref[...]

    def rope(z):
        return z * cos + pltpu.roll(z, LANES - ROT_DIM // 2, 1) * sa + pltpu.roll(z, ROT_DIM // 2, 1) * sb

    u_ref[...] = seg(0, POOL_WIDTH)
    zq = seg(O_Q, Q_WIDTH)
    for i in range(Q_WIDTH // LANES):
        sl = slice(i * LANES, (i + 1) * LANES)
        q_ref[:, sl] = (rope(zq[:, sl]) * Q_SCALE).astype(BF16)
    zk = seg(O_K, K_WIDTH)
    for i in range(K_WIDTH // LANES):
        sl = slice(i * LANES, (i + 1) * LANES)
        kr = rope(zk[:, sl])
        if k_transposed:
            kf_ref[sl, :] = kr.T
        else:
            kf_ref[:, sl] = kr
        kb_ref[:, sl] = kr.astype(BF16)
    zv = seg(O_V, V_WIDTH)
    for h in range(N_KV_HEADS):
        vf_ref[pl.ds(h, tm, stride=N_KV_HEADS), :] = zv[:, h * V_DIM:(h + 1) * V_DIM]
        vb_ref[:, 2 * h * V_DIM:(2 * h + 1) * V_DIM] = zv[:, h * V_DIM:(h + 1) * V_DIM].astype(BF16)
        vb_ref[:, (2 * h + 1) * V_DIM:(2 * h + 2) * V_DIM] = jnp.ones((tm, V_DIM), BF16)
    gate_ref[...] = seg(O_G, 2 * D_MODEL)


def _proj(x3d, g, w_bf, cos, sa, sb, tm, k_transposed):
    b, t, _ = x3d.shape
    row = lambda bi, i: (bi, i, 0)
    tab = lambda bi, i: (i, 0)

    def rows(w, dt):
        return (pl.BlockSpec((None, tm, w), row), jax.ShapeDtypeStruct((b, t, w), dt))

    if k_transposed:
        k_out = (pl.BlockSpec((None, K_WIDTH, tm), lambda bi, i: (bi, 0, i)),
                 jax.ShapeDtypeStruct((b, K_WIDTH, t), F32))
    else:
        k_out = rows(K_WIDTH, F32)
    v_out = (pl.BlockSpec((None, tm * N_KV_HEADS, V_DIM), row),
             jax.ShapeDtypeStruct((b, t * N_KV_HEADS, V_DIM), F32))
    outs = [rows(POOL_WIDTH, F32), rows(Q_WIDTH, BF16), k_out, rows(K_WIDTH, BF16),
            v_out, rows(2 * V_WIDTH, BF16), rows(2 * D_MODEL, F32)]
    return pl.pallas_call(
        functools.partial(_proj_kernel, k_transposed=k_transposed),
        grid=(b, t // tm),
        in_specs=[pl.BlockSpec((None, tm, D_MODEL), row), _const_spec((1, D_MODEL)),
                  _const_spec((D_MODEL, IN_WIDTH)),
                  pl.BlockSpec((tm, LANES), tab), pl.BlockSpec((tm, LANES), tab),
                  pl.BlockSpec((tm, LANES), tab)],
        out_specs=[o[0] for o in outs],
        out_shape=[o[1] for o in outs],
        compiler_params=_params(2),
        name="proj",
    )(x3d, g, w_bf, cos, sa, sb)


def _rope_tables(pos):
    half = ROT_DIM // 2
    j = jnp.arange(LANES, dtype=jnp.int32) % HEAD_DIM
    inv = ROPE_THETA ** (-(j % half).astype(F32) * 2.0 / ROT_DIM)
    ang = pos.astype(F32)[:, None] * inv[None, :]
    c, s = jnp.cos(ang), jnp.sin(ang)
    cos = jnp.where(j < ROT_DIM, c, 1.0)
    sa = jnp.where(j < half, -s, 0.0)
    sb = jnp.where((j >= half) & (j < ROT_DIM), s, 0.0)
    return cos, sa, sb


def _pool_prompt_kernel(u_ref, hist_ref, o_ref, ext_ref, *, tm):
    i = pl.program_id(1)
    u = u_ref[...]
    ext_ref[0:HIST_ROWS, :] = jnp.where(i == 0, 0.0, hist_ref[...])
    ext_ref[HIST_ROWS:, :] = u
    pos = i * tm + lax.broadcasted_iota(jnp.int32, (tm, 1), 0)
    for g, w in enumerate(POOL_WINDOWS):
        sl = slice(g * POOL_GROUP_WIDTH, (g + 1) * POOL_GROUP_WIDTH)
        tot = u[:, sl]
        for k in range(1, w):
            tot = tot + ext_ref[pl.ds(HIST_ROWS - k, tm), sl]
        cnt = jnp.minimum(pos + 1, w).astype(F32)
        o_ref[:, sl] = tot / cnt - u[:, sl]


def _pool_prompt(u3, tm):
    b, s, _ = u3.shape
    per = tm // HIST_ROWS
    return pl.pallas_call(
        functools.partial(_pool_prompt_kernel, tm=tm),
        grid=(b, s // tm),
        in_specs=[pl.BlockSpec((None, tm, POOL_WIDTH), lambda bi, i: (bi, i, 0)),
                  pl.BlockSpec((None, HIST_ROWS, POOL_WIDTH),
                               lambda bi, i: (bi, jnp.maximum(i * per - 1, 0), 0))],
        out_specs=pl.BlockSpec((None, tm, POOL_WIDTH), lambda bi, i: (bi, i, 0)),
        out_shape=jax.ShapeDtypeStruct(u3.shape, F32),
        scratch_shapes=[pltpu.VMEM((HIST_ROWS + tm, POOL_WIDTH), F32)],
        compiler_params=_params(2),
        name="pool_prompt",
    )(u3, u3)


def _pool_sample_kernel(sp_ref, u_ref, o_ref, st_ref, *, past, n_new):
    def ext(r, sl):
        if r < POOL_HIST:
            return sp_ref[:, r * POOL_WIDTH + sl.start: r * POOL_WIDTH + sl.stop]
        r -= POOL_HIST
        return u_ref[:, r * POOL_WIDTH + sl.start: r * POOL_WIDTH + sl.stop]

    for t in range(n_new):
        for g, w in enumerate(POOL_WINDOWS):
            sl = slice(g * POOL_GROUP_WIDTH, (g + 1) * POOL_GROUP_WIDTH)
            cur = ext(POOL_HIST + t, sl)
            tot = cur
            for k in range(1, w):
                tot = tot + ext(POOL_HIST + t - k, sl)
            cnt = float(min(past + t + 1, w))
            o_ref[:, t * POOL_WIDTH + sl.start: t * POOL_WIDTH + sl.stop] = tot / cnt - cur
    keep = (POOL_HIST - n_new) * POOL_WIDTH
    st_ref[:, :keep] = sp_ref[:, n_new * POOL_WIDTH:]
    st_ref[:, keep:] = u_ref[...]


def _pool_sample(sp2, u2, past, n_new):
    return pl.pallas_call(
        functools.partial(_pool_sample_kernel, past=past, n_new=n_new),
        out_shape=[jax.ShapeDtypeStruct(u2.shape, F32), jax.ShapeDtypeStruct(sp2.shape, F32)],
        compiler_params=pltpu.CompilerParams(vmem_limit_bytes=VMEM_LIMIT),
        name="pool_sample",
    )(sp2, u2)


def _lambda(lp_ref, lam_init):
    lp = lp_ref[...]
    a = jnp.sum(lp[0:1] * lp[1:2], axis=-1, keepdims=True)
    b = jnp.sum(lp[2:3] * lp[3:4], axis=-1, keepdims=True)
    return jnp.exp(a) - jnp.exp(b) + lam_init


def _sample_tile_update(s, v_of_head, m_ref, l_ref, acc_ref, hrows):
    m_prev = m_ref[...]
    m_next = jnp.maximum(m_prev, jnp.max(s, axis=1, keepdims=True))
    p = jnp.exp2(s - _lane_tile(m_next, s.shape[1] // LANES))
    alpha = jnp.exp2(m_prev - m_next)
    l_ref[...] = alpha * l_ref[...] + jnp.sum(p, axis=1, keepdims=True)
    p = p.astype(BF16)
    for h in range(N_KV_HEADS):
        hs = slice(h * hrows, (h + 1) * hrows)
        pv = jnp.dot(p[hs], v_of_head(h), preferred_element_type=F32)
        acc_ref[hs, :] = alpha[hs] * acc_ref[hs, :] + pv
    m_ref[...] = m_next


def _attn_kernel(pt_ref, lp_ref, q_ref, k_ref, v_ref, sq_ref, kn_ref, vn_ref, *rest,
                 tq, tk, n_pages_step, steps_per_seq, n_new, lam_init):
    (kt_hbm, v_hbm, o_ref, so_ref, qs_ref, s_ref, m_ref, acc_ref,
     qall_ref, new_ref, sm_ref, sl_ref, sacc_ref, kbuf, vbuf, sem) = rest
    qi = pl.program_id(2)
    flat = (pl.program_id(0) * N_KV_HEADS + pl.program_id(1)) * pl.num_programs(2) + qi
    n_steps = pl.num_programs(0) * N_KV_HEADS * pl.num_programs(2)
    grp = lax.rem(flat, steps_per_seq)
    slot = lax.rem(flat, 2)

    def page_copies(step, to_slot):
        copies = []
        for n in range(n_pages_step):
            page = pt_ref[step * n_pages_step + n]
            copies.append(pltpu.make_async_copy(kt_hbm.at[page], kbuf.at[to_slot, n],
                                                sem.at[0, to_slot]))
            copies.append(pltpu.make_async_copy(v_hbm.at[page], vbuf.at[to_slot, n],
                                                sem.at[1, to_slot]))
        return copies

    @pl.when(flat == 0)
    def _():
        for c in page_copies(0, 0):
            c.start()

    @pl.when(flat + 1 < n_steps)
    def _():
        for c in page_copies(flat + 1, 1 - slot):
            c.start()

    for c in page_copies(flat, slot):
        c.wait()
    half = GROUP * n_new
    hrows = 2 * half
    srows = N_KV_HEADS * hrows

    @pl.when(grp == 0)
    def _():
        qall_ref[...] = jnp.zeros(qall_ref.shape, F32)
        lane = lax.broadcasted_iota(jnp.int32, (half, LANES), 1)
        zero = jnp.zeros((half, LANES), F32)
        for h in range(N_KV_HEADS):
            qh = sq_ref[h]
            r0 = h * hrows
            qall_ref[r0:r0 + half, h * LANES:(h + 1) * LANES] = jnp.where(lane < HEAD_DIM, qh, zero)
            qall_ref[r0 + half:r0 + hrows, h * LANES:(h + 1) * LANES] = \
                jnp.where(lane >= HEAD_DIM, qh, zero)
        sm_ref[...] = jnp.full(sm_ref.shape, NEG, F32)
        sl_ref[...] = jnp.zeros(sl_ref.shape, F32)
        sacc_ref[...] = jnp.zeros(sacc_ref.shape, F32)
        new_ref[...] = jnp.zeros(new_ref.shape, F32)
        new_ref[0, 0:n_new, :] = kn_ref[...]
        new_ref[1, 0:n_new, :] = vn_ref[...]
        s = lax.dot_general(qall_ref[...].astype(BF16), new_ref[0].astype(BF16),
                            (((1,), (1,)), ((), ())), preferred_element_type=F32)
        t_row = lax.rem(lax.broadcasted_iota(jnp.int32, (srows, LANES), 0), n_new)
        t_key = lax.broadcasted_iota(jnp.int32, (srows, LANES), 1)
        s = jnp.where(t_key <= t_row, s, NEG)
        _sample_tile_update(s, lambda h: new_ref[1, :, h * V_DIM:(h + 1) * V_DIM].astype(BF16),
                            sm_ref, sl_ref, sacc_ref, hrows)

    rows = 2 * GROUP * tq
    lane = lax.broadcasted_iota(jnp.int32, (tq, LANES), 1)
    zero = jnp.zeros((tq, LANES), BF16)
    for g in range(GROUP):
        qg = q_ref[:, g * LANES:(g + 1) * LANES]
        qs_ref[(2 * g) * tq:(2 * g + 1) * tq, :] = jnp.where(lane < HEAD_DIM, qg, zero)
        qs_ref[(2 * g + 1) * tq:(2 * g + 2) * tq, :] = jnp.where(lane >= HEAD_DIM, qg, zero)
    m_ref[...] = jnp.full((rows, LANES), NEG, F32)
    acc_ref[...] = jnp.zeros((rows, 2 * V_DIM), F32)

    def scores(j, slot, width=tk):
        start = pl.multiple_of(j * tk, tk)
        s_ref[slot, :, :width] = lax.dot_general(
            qs_ref[...], k_ref[pl.ds(start, width), :], (((1,), (1,)), ((), ())),
            preferred_element_type=F32)

    def accumulate(j, slot, masked, width=tk):
        start = pl.multiple_of(j * tk, tk)
        s = s_ref[slot, :, :width]
        if masked:
            r = lax.broadcasted_iota(jnp.int32, (rows, width), 0)
            qpos = qi * tq + lax.rem(r, tq)
            kpos = start + lax.broadcasted_iota(jnp.int32, (rows, width), 1)
            s = jnp.where(kpos <= qpos, s, NEG)
        m_prev = m_ref[...]
        m_next = jnp.maximum(m_prev, jnp.max(s, axis=1, keepdims=True))
        p = jnp.exp2(s - _lane_tile(m_next, width // LANES)).astype(BF16)
        alpha = jnp.exp2(m_prev - m_next)
        pv = jnp.dot(p, v_ref[pl.ds(start, width), :], preferred_element_type=F32)
        acc_ref[...] = _lane_tile(alpha, 2) * acc_ref[...] + pv
        m_ref[...] = m_next

    n = ((qi + 1) * tq + tk - 1) // tk
    n_pairs = (n - 1) // 2
    scores(0, 0)

    qall = qall_ref[...].astype(BF16)
    s_pages = jnp.concatenate(
        [jnp.dot(qall, kbuf[slot, n].astype(BF16), preferred_element_type=F32)
         for n in range(n_pages_step)], axis=1)

    def page_values(h):
        return jnp.concatenate(
            [vbuf[slot, n, pl.ds(h, PAGE_SIZE, stride=N_KV_HEADS), :].astype(BF16)
             for n in range(n_pages_step)], axis=0)

    _sample_tile_update(s_pages, page_values, sm_ref, sl_ref, sacc_ref, hrows)

    @pl.when(grp == steps_per_seq - 1)
    def _():
        lam_s = _lambda(lp_ref, lam_init)
        o_s = sacc_ref[...] / sl_ref[...]
        for h in range(N_KV_HEADS):
            r0 = h * hrows
            so_ref[h] = o_s[r0:r0 + half] - lam_s * o_s[r0 + half:r0 + hrows]

    def pair_body(jj, c):
        j = 2 * jj
        scores(j + 1, 1)
        accumulate(j, 0, False)
        scores(j + 2, 0)
        accumulate(j + 1, 1, False)
        return c

    lax.fori_loop(0, n_pairs, pair_body, 0)
    odd_tail = n - 2 * n_pairs == 1
    sub = lax.rem(qi, tk // tq)
    for w in range(tk // tq):
        width = (w + 1) * tq

        @pl.when(jnp.logical_and(odd_tail, sub == w))
        def _():
            accumulate(n - 1, 0, True, width)

        @pl.when(jnp.logical_and(jnp.logical_not(odd_tail), sub == w))
        def _():
            scores(n - 1, 1, width)
            accumulate(n - 2, 0, False)
            accumulate(n - 1, 1, True, width)

    lam = _lambda(lp_ref, lam_init)
    acc = acc_ref[...]
    o = acc[:, :V_DIM] / acc[:, V_DIM:]
    for g in range(GROUP):
        o0 = o[(2 * g) * tq:(2 * g + 1) * tq]
        o1 = o[(2 * g + 1) * tq:(2 * g + 2) * tq]
        o_ref[:, g * V_DIM:(g + 1) * V_DIM] = o0 - lam * o1


def _attention(page_table, lp, q3, k3, v3, sq4, kn3, vn3, kt_pages, v_pages, lam_init, tq, tk):
    b, s, _ = q3.shape
    n_seq, n_pages = page_table.shape
    n_new = kn3.shape[1]
    nq = s // tq
    n_steps = b * N_KV_HEADS * nq
    steps_per_seq, rem = divmod(n_steps, n_seq)
    assert rem == 0 and n_pages % steps_per_seq == 0, (n_steps, n_seq, n_pages)
    n_pages_step = n_pages // steps_per_seq
    rows = 2 * GROUP * tq
    srows = N_KV_HEADS * 2 * GROUP * n_new
    qw = GROUP * 2 * HEAD_DIM
    once = pl.Buffered(1)

    def seq_of(bi, h, i):
        return ((bi * N_KV_HEADS + h) * nq + i) // steps_per_seq

    step_pages = page_table.reshape(n_steps * n_pages_step)
    page_buf = pltpu.VMEM((2, n_pages_step) + kt_pages.shape[1:], F32)

    sample_block = lambda bi, h, i, pt: (seq_of(bi, h, i), 0, 0, 0)
    new_block = lambda bi, h, i, pt: (seq_of(bi, h, i), 0, 0)
    grid_spec = pltpu.PrefetchScalarGridSpec(
        num_scalar_prefetch=1,
        grid=(b, N_KV_HEADS, nq),
        in_specs=[pl.BlockSpec(lp.shape, lambda bi, h, i, pt: (0, 0)),
                  pl.BlockSpec((None, tq, qw), lambda bi, h, i, pt: (bi, i, h)),
                  pl.BlockSpec((None, s, LANES), lambda bi, h, i, pt: (bi, 0, h),
                               pipeline_mode=once),
                  pl.BlockSpec((None, s, 2 * V_DIM), lambda bi, h, i, pt: (bi, 0, h),
                               pipeline_mode=once),
                  pl.BlockSpec((None,) + sq4.shape[1:], sample_block),
                  pl.BlockSpec((None, n_new, K_WIDTH), new_block),
                  pl.BlockSpec((None, n_new, V_WIDTH), new_block),
                  pl.BlockSpec(memory_space=pl.ANY), pl.BlockSpec(memory_space=pl.ANY)],
        out_specs=[pl.BlockSpec((None, tq, GROUP * V_DIM), lambda bi, h, i, pt: (bi, i, h)),
                   pl.BlockSpec((None,) + sq4.shape[1:], sample_block)],
        scratch_shapes=[pltpu.VMEM((rows, LANES), BF16), pltpu.VMEM((2, rows, tk), F32),
                        pltpu.VMEM((rows, LANES), F32), pltpu.VMEM((rows, 2 * V_DIM), F32),
                        pltpu.VMEM((srows, K_WIDTH), F32),
                        pltpu.VMEM((2, PAGE_SIZE, K_WIDTH), F32),
                        pltpu.VMEM((srows, LANES), F32), pltpu.VMEM((srows, LANES), F32),
                        pltpu.VMEM((srows, V_DIM), F32),
                        page_buf, page_buf, pltpu.SemaphoreType.DMA((2, 2))])
    return pl.pallas_call(
        functools.partial(_attn_kernel, tq=tq, tk=tk, n_pages_step=n_pages_step,
                          steps_per_seq=steps_per_seq, n_new=n_new, lam_init=lam_init),
        grid_spec=grid_spec,
        out_shape=[jax.ShapeDtypeStruct((b, s, ATTN_WIDTH), F32),
                   jax.ShapeDtypeStruct(sq4.shape, F32)],
        compiler_params=_params(3),
        name="attention",
    )(step_pages, lp, q3, k3, v3, sq4, kn3, vn3, kt_pages, v_pages)


def _merge_kernel(x_ref, pooled_ref, attn_ref, gate_ref, wpg_ref, ps_ref, sg_ref,
                  wpp_ref, wap_ref, wo_ref, o_ref, *, lam_init):
    pooled = pooled_ref[...].astype(BF16)
    mixed = [jnp.dot(pooled[:, g * POOL_GROUP_WIDTH:(g + 1) * POOL_GROUP_WIDTH], wpg_ref[g],
                     preferred_element_type=F32) for g in range(N_POOL_GROUPS)]
    pool_out = jnp.concatenate(mixed, axis=1) * ps_ref[...]
    a_branch = jnp.dot(pool_out.astype(BF16), wpp_ref[...], preferred_element_type=F32)
    sg = sg_ref[...]
    heads = []
    for h in range(N_HEADS):
        oh = attn_ref[:, h * V_DIM:(h + 1) * V_DIM]
        heads.append((_rms(oh, sg) * (1.0 - lam_init)).astype(BF16))
    b_branch = jnp.dot(jnp.concatenate(heads, axis=1), wap_ref[...], preferred_element_type=F32)
    ga = gate_ref[:, :D_MODEL]
    gb = gate_ref[:, D_MODEL:]
    mix = jax.nn.sigmoid(ga) * a_branch + jax.nn.sigmoid(gb) * b_branch
    o_ref[...] = x_ref[...] + jnp.dot(mix.astype(BF16), wo_ref[...], preferred_element_type=F32)


def _merge(x2d, pooled, attn, gates, wpg, ps, sg, wpp, wap, wo, lam_init, tm):
    n = x2d.shape[0]
    row = lambda i: (i, 0)
    return pl.pallas_call(
        functools.partial(_merge_kernel, lam_init=lam_init),
        grid=(n // tm,),
        in_specs=[pl.BlockSpec((tm, D_MODEL), row), pl.BlockSpec((tm, POOL_WIDTH), row),
                  pl.BlockSpec((tm, ATTN_WIDTH), row), pl.BlockSpec((tm, 2 * D_MODEL), row),
                  _const_spec(wpg.shape), _const_spec(ps.shape), _const_spec(sg.shape),
                  _const_spec(wpp.shape), _const_spec(wap.shape), _const_spec(wo.shape)],
        out_specs=pl.BlockSpec((tm, D_MODEL), row),
        out_shape=jax.ShapeDtypeStruct((n, D_MODEL), F32),
        compiler_params=_params(1),
        name="merge",
    )(x2d, pooled, attn, gates, wpg, ps, sg, wpp, wap, wo)


def _ffn_kernel(x_ref, nf_ref, wg_ref, wu_ref, wd_ref, fn_ref, o_ref):
    x = x_ref[...]
    h = _rms(x, nf_ref[...]).astype(BF16)
    gate = jnp.dot(h, wg_ref[...], preferred_element_type=F32)
    up = jnp.dot(h, wu_ref[...], preferred_element_type=F32)
    act = (jax.nn.silu(gate) * up).astype(BF16)
    y = x + jnp.dot(act, wd_ref[...], preferred_element_type=F32)
    o_ref[...] = _rms(y, fn_ref[...])


def _ffn(x2d, nf, wg, wu, wd, fn, tm):
    n = x2d.shape[0]
    row = lambda i: (i, 0)
    return pl.pallas_call(
        _ffn_kernel,
        grid=(n // tm,),
        in_specs=[pl.BlockSpec((tm, D_MODEL), row), _const_spec(nf.shape), _const_spec(wg.shape),
                  _const_spec(wu.shape), _const_spec(wd.shape), _const_spec(fn.shape)],
        out_specs=pl.BlockSpec((tm, D_MODEL), row),
        out_shape=jax.ShapeDtypeStruct((n, D_MODEL), F32),
        compiler_params=_params(1),
        name="ffn",
    )(x2d, nf, wg, wu, wd, fn)


def kernel(x_prompt, x_sample, cache_k, cache_v, state_pool, page_table, norm_mix, w_in,
           w_pool_group, pool_scale, lambda_q1, lambda_k1, lambda_q2, lambda_k2, subln_g,
           w_pool_proj, w_attn_proj, w_out, norm_ffn, w_gate, w_up, w_down, final_norm):
    depth = w_in.shape[0]
    assert depth == 1, "single-layer step"
    batch, seq, _ = x_prompt.shape
    dec_batch, dec_seq, _ = x_sample.shape
    n_pages = page_table.shape[1]
    past = n_pages * PAGE_SIZE
    n_pool = cache_k.shape[1]
    l = 0
    lam_init = 0.8 - 0.6 * math.exp(-0.3 * l)

    tm = 512
    row2 = lambda a: a.reshape(1, -1)
    w_in_bf = w_in[l].astype(BF16)
    wpg = w_pool_group[l].astype(BF16)
    wpp = w_pool_proj[l].astype(BF16)
    wap = w_attn_proj[l].astype(BF16)
    wo = w_out[l].astype(BF16)
    wg = w_gate[l].astype(BF16)
    wu = w_up[l].astype(BF16)
    wd = w_down[l].astype(BF16)
    g_mix, ps, sg = row2(norm_mix[l]), row2(pool_scale[l]), row2(subln_g[l])
    nf, fn = row2(norm_ffn[l]), row2(final_norm)
    lp = jnp.stack([lambda_q1[l], lambda_k1[l], lambda_q2[l], lambda_k2[l]]).astype(F32)

    def tail(x2d, pooled, attn, gates, tile):
        x1 = _merge(x2d, pooled, attn, gates, wpg, ps, sg, wpp, wap, wo, lam_init, tile)
        return _ffn(x1, nf, wg, wu, wd, fn, tile)

    n_p = batch * seq
    n_s = dec_batch * dec_seq
    tabs_p = _rope_tables(jnp.arange(seq, dtype=jnp.int32))
    tabs_s = [jnp.tile(t, (dec_batch, 1))
              for t in _rope_tables(past + jnp.arange(dec_seq, dtype=jnp.int32))]
    u3, q_p, kt, kb, vf_p, vb, gates_p = _proj(x_prompt, g_mix, w_in_bf, *tabs_p, tm, True)
    u_s, q_s, kf_s, _, vf_s, _, gates_s = _proj(x_sample.reshape(1, n_s, D_MODEL), g_mix, w_in_bf,
                                                *tabs_s, n_s, False)

    q4 = q_s.reshape(dec_batch, dec_seq, N_KV_HEADS, GROUP, V_DIM).transpose(0, 2, 3, 1, 4)
    q4 = q4.reshape(dec_batch, N_KV_HEADS, GROUP * dec_seq, V_DIM).astype(F32)
    kt_pages = cache_k.transpose(0, 1, 3, 4, 5, 2).reshape(n_pool, K_WIDTH, PAGE_SIZE)
    v_pages = cache_v.reshape(n_pool, PAGE_SIZE * N_KV_HEADS, V_DIM)
    attn_p, o4 = _attention(page_table, lp, q_p, kb, vb, q4,
                            kf_s.reshape(dec_batch, dec_seq, K_WIDTH),
                            vf_s.reshape(dec_batch, dec_seq, V_WIDTH),
                            kt_pages, v_pages, lam_init, 256, 1024)
    attn_s = o4.reshape(dec_batch, N_KV_HEADS, GROUP, dec_seq, V_DIM).transpose(0, 3, 1, 2, 4)

    pooled_p = _pool_prompt(u3, tm).reshape(n_p, POOL_WIDTH)
    y_prompt = tail(x_prompt.reshape(n_p, D_MODEL), pooled_p, attn_p.reshape(n_p, ATTN_WIDTH),
                    gates_p.reshape(n_p, 2 * D_MODEL), tm).reshape(x_prompt.shape)
    k_prompt = kt.reshape(1, batch, N_KV_HEADS, 2, HEAD_DIM, seq).transpose(0, 1, 5, 2, 3, 4)
    v_prompt = vf_p.reshape(1, batch, seq, N_KV_HEADS, V_DIM)
    pool_prompt = u3[:, seq - POOL_HIST:, :][None]

    pooled_s, pool_state = _pool_sample(
        state_pool[l].reshape(dec_batch, POOL_HIST * POOL_WIDTH),
        u_s.reshape(dec_batch, dec_seq * POOL_WIDTH), past, dec_seq)
    y_sample = tail(x_sample.reshape(n_s, D_MODEL), pooled_s.reshape(n_s, POOL_WIDTH),
                    attn_s.reshape(n_s, ATTN_WIDTH), gates_s.reshape(n_s, 2 * D_MODEL),
                    n_s).reshape(x_sample.shape)
    k_sample = kf_s.reshape(1, dec_batch, dec_seq, N_KV_HEADS, 2, HEAD_DIM)
    v_sample = vf_s.reshape(1, dec_batch, dec_seq, N_KV_HEADS, V_DIM)
    pool_sample = pool_state.reshape(1, dec_batch, POOL_HIST, POOL_WIDTH)

    return (y_prompt, y_sample, k_prompt, v_prompt, pool_prompt,
            k_sample, v_sample, pool_sample)
```

```python
import functools
import math

import jax
import jax.numpy as jnp
from jax import lax
from jax.experimental import pallas as pl
from jax.experimental.pallas import tpu as pltpu

D_MODEL = 1024
PAGE_SIZE = 128
POOL_WINDOWS = (2, 4, 8, 16)
N_POOL_GROUPS = len(POOL_WINDOWS)
POOL_GROUP_WIDTH = D_MODEL // 8
POOL_WIDTH = N_POOL_GROUPS * POOL_GROUP_WIDTH
POOL_HIST = max(POOL_WINDOWS) - 1
N_HEADS = 8
N_KV_HEADS = 4
GROUP = N_HEADS // N_KV_HEADS
HEAD_DIM = D_MODEL // N_HEADS // 2
V_DIM = 2 * HEAD_DIM
ROT_DIM = HEAD_DIM // 4
ROPE_THETA = 500000.0
Q_WIDTH = N_HEADS * 2 * HEAD_DIM
K_WIDTH = N_KV_HEADS * 2 * HEAD_DIM
V_WIDTH = N_KV_HEADS * V_DIM
ATTN_WIDTH = N_HEADS * V_DIM
D_FF = ((8 * D_MODEL + 3 * 256 - 1) // (3 * 256)) * 256
EPS = 1e-6
NEG = -1e30
Q_SCALE = HEAD_DIM ** -0.5 * math.log2(math.e)

LANES = 128
HIST_ROWS = 16
VMEM_LIMIT = 56 * 1024 * 1024

F32 = jnp.float32
BF16 = jnp.bfloat16

O_Q = POOL_WIDTH
O_K = O_Q + Q_WIDTH
O_V = O_K + K_WIDTH
O_G = O_V + V_WIDTH
IN_WIDTH = O_G + 2 * D_MODEL


def _const_spec(shape):
    zeros = (0,) * len(shape)
    return pl.BlockSpec(shape, lambda *_: zeros, pipeline_mode=pl.Buffered(1))


def _params(n_grid):
    return pltpu.CompilerParams(
        dimension_semantics=("arbitrary",) * n_grid, vmem_limit_bytes=VMEM_LIMIT)


def _lane_tile(x, n):
    return x if n == 1 else jnp.concatenate([x] * n, axis=1)


def _rms(x, g):
    return x * lax.rsqrt(jnp.mean(x * x, axis=-1, keepdims=True) + EPS) * g


def _proj_kernel(x_ref, g_ref, w_ref, cos_ref, sa_ref, sb_ref,
                 u_ref, q_ref, kf_ref, kb_ref, vf_ref, vb_ref, gate_ref, *, k_transposed):
    tm = x_ref.shape[0]
    xn = _rms(x_ref[...], g_ref[...]).astype(BF16)

    def seg(lo, width):
        return jnp.dot(xn, w_ref[:, lo:lo + width], preferred_element_type=F32)

    cos, sa, sb = cos_ref[...], sa_ref[...], sb_ref[...]

    def rope(z):
        return z * cos + pltpu.roll(z, LANES - ROT_DIM // 2, 1) * sa + pltpu.roll(z, ROT_DIM // 2, 1) * sb

    u_ref[...] = seg(0, POOL_WIDTH)
    zq = seg(O_Q, Q_WIDTH)
    for i in range(Q_WIDTH // LANES):
        sl = slice(i * LANES, (i + 1) * LANES)
        q_ref[:, sl] = (rope(zq[:, sl]) * Q_SCALE).astype(BF16)
    zk = seg(O_K, K_WIDTH)
    for i in range(K_WIDTH // LANES):
        sl = slice(i * LANES, (i + 1) * LANES)
        kr = rope(zk[:, sl])
        if k_transposed:
            kf_ref[sl, :] = kr.T
        else:
            kf_ref[:, sl] = kr
        kb_ref[:, sl] = kr.astype(BF16)
    zv = seg(O_V, V_WIDTH)
    for h in range(N_KV_HEADS):
        vf_ref[pl.ds(h, tm, stride=N_KV_HEADS), :] = zv[:, h * V_DIM:(h + 1) * V_DIM]
        vb_ref[:, 2 * h * V_DIM:(2 * h + 1) * V_DIM] = zv[:, h * V_DIM:(h + 1) * V_DIM].astype(BF16)
        vb_ref[:, (2 * h + 1) * V_DIM:(2 * h + 2) * V_DIM] = jnp.ones((tm, V_DIM), BF16)
    gate_ref[...] = seg(O_G, 2 * D_MODEL)


def _proj(x3d, g, w_bf, cos, sa, sb, tm, k_transposed):
    b, t, _ = x3d.shape
    row = lambda bi, i: (bi, i, 0)
    tab = lambda bi, i: (i, 0)

    def rows(w, dt):
        return (pl.BlockSpec((None, tm, w), row), jax.ShapeDtypeStruct((b, t, w), dt))

    if k_transposed:
        k_out = (pl.BlockSpec((None, K_WIDTH, tm), lambda bi, i: (bi, 0, i)),
                 jax.ShapeDtypeStruct((b, K_WIDTH, t), F32))
    else:
        k_out = rows(K_WIDTH, F32)
    v_out = (pl.BlockSpec((None, tm * N_KV_HEADS, V_DIM), row),
             jax.ShapeDtypeStruct((b, t * N_KV_HEADS, V_DIM), F32))
    outs = [rows(POOL_WIDTH, F32), rows(Q_WIDTH, BF16), k_out, rows(K_WIDTH, BF16),
            v_out, rows(2 * V_WIDTH, BF16), rows(2 * D_MODEL, F32)]
    return pl.pallas_call(
        functools.partial(_proj_kernel, k_transposed=k_transposed),
        grid=(b, t // tm),
        in_specs=[pl.BlockSpec((None, tm, D_MODEL), row), _const_spec((1, D_MODEL)),
                  _const_spec((D_MODEL, IN_WIDTH)),
                  pl.BlockSpec((tm, LANES), tab), pl.BlockSpec((tm, LANES), tab),
                  pl.BlockSpec((tm, LANES), tab)],
        out_specs=[o[0] for o in outs],
        out_shape=[o[1] for o in outs],
        compiler_params=_params(2),
        name="proj",
    )(x3d, g, w_bf, cos, sa, sb)


def _rope_tables(pos):
    half = ROT_DIM // 2
    j = jnp.arange(LANES, dtype=jnp.int32) % HEAD_DIM
    inv = ROPE_THETA ** (-(j % half).astype(F32) * 2.0 / ROT_DIM)
    ang = pos.astype(F32)[:, None] * inv[None, :]
    c, s = jnp.cos(ang), jnp.sin(ang)
    cos = jnp.where(j < ROT_DIM, c, 1.0)
    sa = jnp.where(j < half, -s, 0.0)
    sb = jnp.where((j >= half) & (j < ROT_DIM), s, 0.0)
    return cos, sa, sb


def _pool_prompt_kernel(u_ref, hist_ref, o_ref, ext_ref, *, tm):
    i = pl.program_id(1)
    u = u_ref[...]
    ext_ref[0:HIST_ROWS, :] = jnp.where(i == 0, 0.0, hist_ref[...])
    ext_ref[HIST_ROWS:, :] = u
    pos = i * tm + lax.broadcasted_iota(jnp.int32, (tm, 1), 0)
    for g, w in enumerate(POOL_WINDOWS):
        sl = slice(g * POOL_GROUP_WIDTH, (g + 1) * POOL_GROUP_WIDTH)
        tot = u[:, sl]
        for k in range(1, w):
            tot = tot + ext_ref[pl.ds(HIST_ROWS - k, tm), sl]
        cnt = jnp.minimum(pos + 1, w).astype(F32)
        o_ref[:, sl] = tot / cnt - u[:, sl]


def _pool_prompt(u3, tm):
    b, s, _ = u3.shape
    per = tm // HIST_ROWS
    return pl.pallas_call(
        functools.partial(_pool_prompt_kernel, tm=tm),
        grid=(b, s // tm),
        in_specs=[pl.BlockSpec((None, tm, POOL_WIDTH), lambda bi, i: (bi, i, 0)),
                  pl.BlockSpec((None, HIST_ROWS, POOL_WIDTH),
                               lambda bi, i: (bi, jnp.maximum(i * per - 1, 0), 0))],
        out_specs=pl.BlockSpec((None, tm, POOL_WIDTH), lambda bi, i: (bi, i, 0)),
        out_shape=jax.ShapeDtypeStruct(u3.shape, F32),
        scratch_shapes=[pltpu.VMEM((HIST_ROWS + tm, POOL_WIDTH), F32)],
        compiler_params=_params(2),
        name="pool_prompt",
    )(u3, u3)


def _pool_sample_kernel(sp_ref, u_ref, o_ref, st_ref, *, past, n_new):
    def ext(r, sl):
        if r < POOL_HIST:
            return sp_ref[:, r * POOL_WIDTH + sl.start: r * POOL_WIDTH + sl.stop]
        r -= POOL_HIST
        return u_ref[:, r * POOL_WIDTH + sl.start: r * POOL_WIDTH + sl.stop]

    for t in range(n_new):
        for g, w in enumerate(POOL_WINDOWS):
            sl = slice(g * POOL_GROUP_WIDTH, (g + 1) * POOL_GROUP_WIDTH)
            cur = ext(POOL_HIST + t, sl)
            tot = cur
            for k in range(1, w):
                tot = tot + ext(POOL_HIST + t - k, sl)
            cnt = float(min(past + t + 1, w))
            o_ref[:, t * POOL_WIDTH + sl.start: t * POOL_WIDTH + sl.stop] = tot / cnt - cur
    keep = (POOL_HIST - n_new) * POOL_WIDTH
    st_ref[:, :keep] = sp_ref[:, n_new * POOL_WIDTH:]
    st_ref[:, keep:] = u_ref[...]


def _pool_sample(sp2, u2, past, n_new):
    return pl.pallas_call(
        functools.partial(_pool_sample_kernel, past=past, n_new=n_new),
        out_shape=[jax.ShapeDtypeStruct(u2.shape, F32), jax.ShapeDtypeStruct(sp2.shape, F32)],
        compiler_params=pltpu.CompilerParams(vmem_limit_bytes=VMEM_LIMIT),
        name="pool_sample",
    )(sp2, u2)


def _lambda(lp_ref, lam_init):
    lp = lp_ref[...]
    a = jnp.sum(lp[0:1] * lp[1:2], axis=-1, keepdims=True)
    b = jnp.sum(lp[2:3] * lp[3:4], axis=-1, keepdims=True)
    return jnp.exp(a) - jnp.exp(b) + lam_init


def _sample_tile_update(s, v_all, m_ref, l_ref, acc_ref, hrows):
    m_prev = m_ref[...]
    m_next = jnp.maximum(m_prev, jnp.max(s, axis=1, keepdims=True))
    p = jnp.exp2(s - _lane_tile(m_next, s.shape[1] // LANES))
    alpha = jnp.exp2(m_prev - m_next)
    l_ref[...] = alpha * l_ref[...] + jnp.sum(p, axis=1, keepdims=True)
    pv = jnp.dot(p.astype(BF16), v_all, preferred_element_type=F32)
    for h in range(N_KV_HEADS):
        hs = slice(h * hrows, (h + 1) * hrows)
        acc_ref[hs, :] = alpha[hs] * acc_ref[hs, :] + pv[hs, h * V_DIM:(h + 1) * V_DIM]
    m_ref[...] = m_next


def _attn_kernel(pt_ref, lp_ref, q_ref, k_ref, v_ref, sq_ref, kn_ref, vn_ref, *rest,
                 tq, tk, n_pages_step, steps_per_seq, n_new, lam_init):
    (kt_hbm, v_hbm, o_ref, so_ref, qs_ref, s_ref, m_ref, acc_ref,
     qall_ref, new_ref, sm_ref, sl_ref, sacc_ref, kbuf, vbuf, sem) = rest
    qi = pl.program_id(2)
    flat = (pl.program_id(0) * N_KV_HEADS + pl.program_id(1)) * pl.num_programs(2) + qi
    n_steps = pl.num_programs(0) * N_KV_HEADS * pl.num_programs(2)
    grp = lax.rem(flat, steps_per_seq)
    slot = lax.rem(flat, 2)

    def page_copies(step, to_slot):
        copies = []
        for n in range(n_pages_step):
            page = pt_ref[step * n_pages_step + n]
            copies.append(pltpu.make_async_copy(kt_hbm.at[page], kbuf.at[to_slot, n],
                                                sem.at[0, to_slot]))
            copies.append(pltpu.make_async_copy(v_hbm.at[page], vbuf.at[to_slot, n],
                                                sem.at[1, to_slot]))
        return copies

    @pl.when(flat == 0)
    def _():
        for c in page_copies(0, 0):
            c.start()

    @pl.when(flat + 1 < n_steps)
    def _():
        for c in page_copies(flat + 1, 1 - slot):
            c.start()

    for c in page_copies(flat, slot):
        c.wait()
    half = GROUP * n_new
    hrows = 2 * half
    srows = N_KV_HEADS * hrows

    @pl.when(grp == 0)
    def _():
        qall_ref[...] = jnp.zeros(qall_ref.shape, F32)
        lane = lax.broadcasted_iota(jnp.int32, (half, LANES), 1)
        zero = jnp.zeros((half, LANES), F32)
        for h in range(N_KV_HEADS):
            qh = sq_ref[h]
            r0 = h * hrows
            qall_ref[r0:r0 + half, h * LANES:(h + 1) * LANES] = jnp.where(lane < HEAD_DIM, qh, zero)
            qall_ref[r0 + half:r0 + hrows, h * LANES:(h + 1) * LANES] = \
                jnp.where(lane >= HEAD_DIM, qh, zero)
        sm_ref[...] = jnp.full(sm_ref.shape, NEG, F32)
        sl_ref[...] = jnp.zeros(sl_ref.shape, F32)
        sacc_ref[...] = jnp.zeros(sacc_ref.shape, F32)
        new_ref[...] = jnp.zeros(new_ref.shape, F32)
        new_ref[0, 0:n_new, :] = kn_ref[...]
        new_ref[1, 0:n_new, :] = vn_ref[...]
        s = lax.dot_general(qall_ref[...].astype(BF16), new_ref[0].astype(BF16),
                            (((1,), (1,)), ((), ())), preferred_element_type=F32)
        t_row = lax.rem(lax.broadcasted_iota(jnp.int32, (srows, LANES), 0), n_new)
        t_key = lax.broadcasted_iota(jnp.int32, (srows, LANES), 1)
        s = jnp.where(t_key <= t_row, s, NEG)
        _sample_tile_update(s, new_ref[1].astype(BF16), sm_ref, sl_ref, sacc_ref, hrows)

    rows = 2 * GROUP * tq
    lane = lax.broadcasted_iota(jnp.int32, (tq, LANES), 1)
    zero = jnp.zeros((tq, LANES), BF16)
    for g in range(GROUP):
        qg = q_ref[:, g * LANES:(g + 1) * LANES]
        qs_ref[(2 * g) * tq:(2 * g + 1) * tq, :] = jnp.where(lane < HEAD_DIM, qg, zero)
        qs_ref[(2 * g + 1) * tq:(2 * g + 2) * tq, :] = jnp.where(lane >= HEAD_DIM, qg, zero)
    m_ref[...] = jnp.full((rows, LANES), NEG, F32)
    acc_ref[...] = jnp.zeros((rows, 2 * V_DIM), F32)

    def scores(j, slot, width=tk):
        start = pl.multiple_of(j * tk, tk)
        s_ref[slot, :, :width] = lax.dot_general(
            qs_ref[...], k_ref[pl.ds(start, width), :], (((1,), (1,)), ((), ())),
            preferred_element_type=F32)

    def accumulate(j, slot, masked, width=tk):
        start = pl.multiple_of(j * tk, tk)
        s = s_ref[slot, :, :width]
        if masked:
            r = lax.broadcasted_iota(jnp.int32, (rows, width), 0)
            qpos = qi * tq + lax.rem(r, tq)
            kpos = start + lax.broadcasted_iota(jnp.int32, (rows, width), 1)
            s = jnp.where(kpos <= qpos, s, NEG)
        m_prev = m_ref[...]
        m_next = jnp.maximum(m_prev, jnp.max(s, axis=1, keepdims=True))
        p = jnp.exp2(s - _lane_tile(m_next, width // LANES)).astype(BF16)
        alpha = jnp.exp2(m_prev - m_next)
        pv = jnp.dot(p, v_ref[pl.ds(start, width), :], preferred_element_type=F32)
        acc_ref[...] = _lane_tile(alpha, 2) * acc_ref[...] + pv
        m_ref[...] = m_next

    n = ((qi + 1) * tq + tk - 1) // tk
    n_pairs = (n - 1) // 2
    scores(0, 0)

    qall = qall_ref[...].astype(BF16)
    k_all = jnp.concatenate([kbuf[slot, n].astype(BF16) for n in range(n_pages_step)], axis=1)
    s_pages = jnp.dot(qall, k_all, preferred_element_type=F32)
    v_all = jnp.concatenate(
        [jnp.concatenate([vbuf[slot, n, pl.ds(h, PAGE_SIZE, stride=N_KV_HEADS), :].astype(BF16)
                          for h in range(N_KV_HEADS)], axis=1)
         for n in range(n_pages_step)], axis=0)
    _sample_tile_update(s_pages, v_all, sm_ref, sl_ref, sacc_ref, hrows)

    @pl.when(grp == steps_per_seq - 1)
    def _():
        lam_s = _lambda(lp_ref, lam_init)
        o_s = sacc_ref[...] / sl_ref[...]
        for h in range(N_KV_HEADS):
            r0 = h * hrows
            so_ref[h] = o_s[r0:r0 + half] - lam_s * o_s[r0 + half:r0 + hrows]

    def pair_body(jj, c):
        j = 2 * jj
        scores(j + 1, 1)
        accumulate(j, 0, False)
        scores(j + 2, 0)
        accumulate(j + 1, 1, False)
        return c

    def quad_body(qq, c):
        pair_body(2 * qq, c)
        return pair_body(2 * qq + 1, c)

    n_quads = n_pairs // 2
    lax.fori_loop(0, n_quads, quad_body, 0)
    lax.fori_loop(2 * n_quads, n_pairs, pair_body, 0)
    odd_tail = n - 2 * n_pairs == 1
    sub = lax.rem(qi, tk // tq)
    for w in range(tk // tq):
        width = (w + 1) * tq

        @pl.when(jnp.logical_and(odd_tail, sub == w))
        def _():
            accumulate(n - 1, 0, True, width)

        @pl.when(jnp.logical_and(jnp.logical_not(odd_tail), sub == w))
        def _():
            scores(n - 1, 1, width)
            accumulate(n - 2, 0, False)
            accumulate(n - 1, 1, True, width)

    lam = _lambda(lp_ref, lam_init)
    acc = acc_ref[...]
    o = acc[:, :V_DIM] / acc[:, V_DIM:]
    for g in range(GROUP):
        o0 = o[(2 * g) * tq:(2 * g + 1) * tq]
        o1 = o[(2 * g + 1) * tq:(2 * g + 2) * tq]
        o_ref[:, g * V_DIM:(g + 1) * V_DIM] = o0 - lam * o1


def _attention(page_table, lp, q3, k3, v3, sq4, kn3, vn3, kt_pages, v_pages, lam_init, tq, tk):
    b, s, _ = q3.shape
    n_seq, n_pages = page_table.shape
    n_new = kn3.shape[1]
    nq = s // tq
    n_steps = b * N_KV_HEADS * nq
    steps_per_seq, rem = divmod(n_steps, n_seq)
    assert rem == 0 and n_pages % steps_per_seq == 0, (n_steps, n_seq, n_pages)
    n_pages_step = n_pages // steps_per_seq
    rows = 2 * GROUP * tq
    srows = N_KV_HEADS * 2 * GROUP * n_new
    qw = GROUP * 2 * HEAD_DIM
    once = pl.Buffered(1)

    def seq_of(bi, h, i):
        return ((bi * N_KV_HEADS + h) * nq + i) // steps_per_seq

    step_pages = page_table.reshape(n_steps * n_pages_step)
    page_buf = pltpu.VMEM((2, n_pages_step) + kt_pages.shape[1:], F32)

    sample_block = lambda bi, h, i, pt: (seq_of(bi, h, i), 0, 0, 0)
    new_block = lambda bi, h, i, pt: (seq_of(bi, h, i), 0, 0)
    grid_spec = pltpu.PrefetchScalarGridSpec(
        num_scalar_prefetch=1,
        grid=(b, N_KV_HEADS, nq),
        in_specs=[pl.BlockSpec(lp.shape, lambda bi, h, i, pt: (0, 0)),
                  pl.BlockSpec((None, tq, qw), lambda bi, h, i, pt: (bi, i, h)),
                  pl.BlockSpec((None, s, LANES), lambda bi, h, i, pt: (bi, 0, h),
                               pipeline_mode=once),
                  pl.BlockSpec((None, s, 2 * V_DIM), lambda bi, h, i, pt: (bi, 0, h),
                               pipeline_mode=once),
                  pl.BlockSpec((None,) + sq4.shape[1:], sample_block),
                  pl.BlockSpec((None, n_new, K_WIDTH), new_block),
                  pl.BlockSpec((None, n_new, V_WIDTH), new_block),
                  pl.BlockSpec(memory_space=pl.ANY), pl.BlockSpec(memory_space=pl.ANY)],
        out_specs=[pl.BlockSpec((None, tq, GROUP * V_DIM), lambda bi, h, i, pt: (bi, i, h)),
                   pl.BlockSpec((None,) + sq4.shape[1:], sample_block)],
        scratch_shapes=[pltpu.VMEM((rows, LANES), BF16), pltpu.VMEM((2, rows, tk), F32),
                        pltpu.VMEM((rows, LANES), F32), pltpu.VMEM((rows, 2 * V_DIM), F32),
                        pltpu.VMEM((srows, K_WIDTH), F32),
                        pltpu.VMEM((2, PAGE_SIZE, K_WIDTH), F32),
                        pltpu.VMEM((srows, LANES), F32), pltpu.VMEM((srows, LANES), F32),
                        pltpu.VMEM((srows, V_DIM), F32),
                        page_buf, page_buf, pltpu.SemaphoreType.DMA((2, 2))])
    return pl.pallas_call(
        functools.partial(_attn_kernel, tq=tq, tk=tk, n_pages_step=n_pages_step,
                          steps_per_seq=steps_per_seq, n_new=n_new, lam_init=lam_init),
        grid_spec=grid_spec,
        out_shape=[jax.ShapeDtypeStruct((b, s, ATTN_WIDTH), F32),
                   jax.ShapeDtypeStruct(sq4.shape, F32)],
        compiler_params=_params(3),
        name="attention",
    )(step_pages, lp, q3, k3, v3, sq4, kn3, vn3, kt_pages, v_pages)


def _merge_kernel(x_ref, pooled_ref, attn_ref, gate_ref, wpg_ref, ps_ref, sg_ref,
                  wpp_ref, wap_ref, wo_ref, o_ref, *, lam_init):
    pooled = pooled_ref[...].astype(BF16)
    mixed = [jnp.dot(pooled[:, g * POOL_GROUP_WIDTH:(g + 1) * POOL_GROUP_WIDTH], wpg_ref[g],
                     preferred_element_type=F32) for g in range(N_POOL_GROUPS)]
    pool_out = jnp.concatenate(mixed, axis=1) * ps_ref[...]
    a_branch = jnp.dot(pool_out.astype(BF16), wpp_ref[...], preferred_element_type=F32)
    sg = sg_ref[...]
    heads = []
    for h in range(N_HEADS):
        oh = attn_ref[:, h * V_DIM:(h + 1) * V_DIM]
        heads.append((_rms(oh, sg) * (1.0 - lam_init)).astype(BF16))
    b_branch = jnp.dot(jnp.concatenate(heads, axis=1), wap_ref[...], preferred_element_type=F32)
    ga = gate_ref[:, :D_MODEL]
    gb = gate_ref[:, D_MODEL:]
    mix = jax.nn.sigmoid(ga) * a_branch + jax.nn.sigmoid(gb) * b_branch
    o_ref[...] = x_ref[...] + jnp.dot(mix.astype(BF16), wo_ref[...], preferred_element_type=F32)


def _merge(x2d, pooled, attn, gates, wpg, ps, sg, wpp, wap, wo, lam_init, tm):
    n = x2d.shape[0]
    row = lambda i: (i, 0)
    return pl.pallas_call(
        functools.partial(_merge_kernel, lam_init=lam_init),
        grid=(n // tm,),
        in_specs=[pl.BlockSpec((tm, D_MODEL), row), pl.BlockSpec((tm, POOL_WIDTH), row),
                  pl.BlockSpec((tm, ATTN_WIDTH), row), pl.BlockSpec((tm, 2 * D_MODEL), row),
                  _const_spec(wpg.shape), _const_spec(ps.shape), _const_spec(sg.shape),
                  _const_spec(wpp.shape), _const_spec(wap.shape), _const_spec(wo.shape)],
        out_specs=pl.BlockSpec((tm, D_MODEL), row),
        out_shape=jax.ShapeDtypeStruct((n, D_MODEL), F32),
        compiler_params=_params(1),
        name="merge",
    )(x2d, pooled, attn, gates, wpg, ps, sg, wpp, wap, wo)


def _ffn_kernel(x_ref, nf_ref, wg_ref, wu_ref, wd_ref, fn_ref, o_ref):
    x = x_ref[...]
    h = _rms(x, nf_ref[...]).astype(BF16)
    gate = jnp.dot(h, wg_ref[...], preferred_element_type=F32)
    up = jnp.dot(h, wu_ref[...], preferred_element_type=F32)
    act = (jax.nn.silu(gate) * up).astype(BF16)
    y = x + jnp.dot(act, wd_ref[...], preferred_element_type=F32)
    o_ref[...] = _rms(y, fn_ref[...])


def _ffn(x2d, nf, wg, wu, wd, fn, tm):
    n = x2d.shape[0]
    row = lambda i: (i, 0)
    return pl.pallas_call(
        _ffn_kernel,
        grid=(n // tm,),
        in_specs=[pl.BlockSpec((tm, D_MODEL), row), _const_spec(nf.shape), _const_spec(wg.shape),
                  _const_spec(wu.shape), _const_spec(wd.shape), _const_spec(fn.shape)],
        out_specs=pl.BlockSpec((tm, D_MODEL), row),
        out_shape=jax.ShapeDtypeStruct((n, D_MODEL), F32),
        compiler_params=_params(1),
        name="ffn",
    )(x2d, nf, wg, wu, wd, fn)


def kernel(x_prompt, x_sample, cache_k, cache_v, state_pool, page_table, norm_mix, w_in,
           w_pool_group, pool_scale, lambda_q1, lambda_k1, lambda_q2, lambda_k2, subln_g,
           w_pool_proj, w_attn_proj, w_out, norm_ffn, w_gate, w_up, w_down, final_norm):
    depth = w_in.shape[0]
    assert depth == 1, "single-layer step"
    batch, seq, _ = x_prompt.shape
    dec_batch, dec_seq, _ = x_sample.shape
    n_pages = page_table.shape[1]
    past = n_pages * PAGE_SIZE
    n_pool = cache_k.shape[1]
    l = 0
    lam_init = 0.8 - 0.6 * math.exp(-0.3 * l)

    tm = 512
    row2 = lambda a: a.reshape(1, -1)
    w_in_bf = w_in[l].astype(BF16)
    wpg = w_pool_group[l].astype(BF16)
    wpp = w_pool_proj[l].astype(BF16)
    wap = w_attn_proj[l].astype(BF16)
    wo = w_out[l].astype(BF16)
    wg = w_gate[l].astype(BF16)
    wu = w_up[l].astype(BF16)
    wd = w_down[l].astype(BF16)
    g_mix, ps, sg = row2(norm_mix[l]), row2(pool_scale[l]), row2(subln_g[l])
    nf, fn = row2(norm_ffn[l]), row2(final_norm)
    lp = jnp.stack([lambda_q1[l], lambda_k1[l], lambda_q2[l], lambda_k2[l]]).astype(F32)

    def tail(x2d, pooled, attn, gates, tile):
        x1 = _merge(x2d, pooled, attn, gates, wpg, ps, sg, wpp, wap, wo, lam_init, tile)
        return _ffn(x1, nf, wg, wu, wd, fn, tile)

    n_p = batch * seq
    n_s = dec_batch * dec_seq
    tabs_p = _rope_tables(jnp.arange(seq, dtype=jnp.int32))
    tabs_s = [jnp.tile(t, (dec_batch, 1))
              for t in _rope_tables(past + jnp.arange(dec_seq, dtype=jnp.int32))]
    u3, q_p, kt, kb, vf_p, vb, gates_p = _proj(x_prompt, g_mix, w_in_bf, *tabs_p, tm, True)
    u_s, q_s, kf_s, _, vf_s, _, gates_s = _proj(x_sample.reshape(1, n_s, D_MODEL), g_mix, w_in_bf,
                                                *tabs_s, n_s, False)

    q4 = q_s.reshape(dec_batch, dec_seq, N_KV_HEADS, GROUP, V_DIM).transpose(0, 2, 3, 1, 4)
    q4 = q4.reshape(dec_batch, N_KV_HEADS, GROUP * dec_seq, V_DIM).astype(F32)
    kt_pages = cache_k.transpose(0, 1, 3, 4, 5, 2).reshape(n_pool, K_WIDTH, PAGE_SIZE)
    v_pages = cache_v.reshape(n_pool, PAGE_SIZE * N_KV_HEADS, V_DIM)
    attn_p, o4 = _attention(page_table, lp, q_p, kb, vb, q4,
                            kf_s.reshape(dec_batch, dec_seq, K_WIDTH),
                            vf_s.reshape(dec_batch, dec_seq, V_WIDTH),
                            kt_pages, v_pages, lam_init, 256, 1024)
    attn_s = o4.reshape(dec_batch, N_KV_HEADS, GROUP, dec_seq, V_DIM).transpose(0, 3, 1, 2, 4)

    pooled_p = _pool_prompt(u3, tm).reshape(n_p, POOL_WIDTH)
    y_prompt = tail(x_prompt.reshape(n_p, D_MODEL), pooled_p, attn_p.reshape(n_p, ATTN_WIDTH),
                    gates_p.reshape(n_p, 2 * D_MODEL), tm).reshape(x_prompt.shape)
    k_prompt = kt.reshape(1, batch, N_KV_HEADS, 2, HEAD_DIM, seq).transpose(0, 1, 5, 2, 3, 4)
    v_prompt = vf_p.reshape(1, batch, seq, N_KV_HEADS, V_DIM)
    pool_prompt = u3[:, seq - POOL_HIST:, :][None]

    pooled_s, pool_state = _pool_sample(
        state_pool[l].reshape(dec_batch, POOL_HIST * POOL_WIDTH),
        u_s.reshape(dec_batch, dec_seq * POOL_WIDTH), past, dec_seq)
    y_sample = tail(x_sample.reshape(n_s, D_MODEL), pooled_s.reshape(n_s, POOL_WIDTH),
                    attn_s.reshape(n_s, ATTN_WIDTH), gates_s.reshape(n_s, 2 * D_MODEL),
                    n_s).reshape(x_sample.shape)
    k_sample = kf_s.reshape(1, dec_batch, dec_seq, N_KV_HEADS, 2, HEAD_DIM)
    v_sample = vf_s.reshape(1, dec_batch, dec_seq, N_KV_HEADS, V_DIM)
    pool_sample = pool_state.reshape(1, dec_batch, POOL_HIST, POOL_WIDTH)

    return (y_prompt, y_sample, k_prompt, v_prompt, pool_prompt,
            k_sample, v_sample, pool_sample)
```

```python
import functools
import math

import jax
import jax.numpy as jnp
from jax import lax
from jax.experimental import pallas as pl
from jax.experimental.pallas import tpu as pltpu

D_MODEL = 1024
PAGE_SIZE = 128
POOL_WINDOWS = (2, 4, 8, 16)
N_POOL_GROUPS = len(POOL_WINDOWS)
POOL_GROUP_WIDTH = D_MODEL // 8
POOL_WIDTH = N_POOL_GROUPS * POOL_GROUP_WIDTH
POOL_HIST = max(POOL_WINDOWS) - 1
N_HEADS = 8
N_KV_HEADS = 4
GROUP = N_HEADS // N_KV_HEADS
HEAD_DIM = D_MODEL // N_HEADS // 2
V_DIM = 2 * HEAD_DIM
ROT_DIM = HEAD_DIM // 4
ROPE_THETA = 500000.0
Q_WIDTH = N_HEADS * 2 * HEAD_DIM
K_WIDTH = N_KV_HEADS * 2 * HEAD_DIM
V_WIDTH = N_KV_HEADS * V_DIM
ATTN_WIDTH = N_HEADS * V_DIM
D_FF = ((8 * D_MODEL + 3 * 256 - 1) // (3 * 256)) * 256
EPS = 1e-6
NEG = -1e30
Q_SCALE = HEAD_DIM ** -0.5 * math.log2(math.e)

LANES = 128
HIST_ROWS = 16
VMEM_LIMIT = 56 * 1024 * 1024

F32 = jnp.float32
BF16 = jnp.bfloat16

O_Q = POOL_WIDTH
O_K = O_Q + Q_WIDTH
O_V = O_K + K_WIDTH
O_G = O_V + V_WIDTH
IN_WIDTH = O_G + 2 * D_MODEL


def _const_spec(shape):
    zeros = (0,) * len(shape)
    return pl.BlockSpec(shape, lambda *_: zeros, pipeline_mode=pl.Buffered(1))


def _params(n_grid):
    return pltpu.CompilerParams(
        dimension_semantics=("arbitrary",) * n_grid, vmem_limit_bytes=VMEM_LIMIT)


def _lane_tile(x, n):
    return x if n == 1 else jnp.concatenate([x] * n, axis=1)


def _rms(x, g):
    return x * lax.rsqrt(jnp.mean(x * x, axis=-1, keepdims=True) + EPS) * g


def _proj_kernel(x_ref, g_ref, w_ref, cos_ref, sa_ref, sb_ref,
                 u_ref, q_ref, kf_ref, kb_ref, vf_ref, vb_ref, gate_ref, *, k_transposed):
    tm = x_ref.shape[0]
    xn = _rms(x_ref[...], g_ref[...]).astype(BF16)

    def seg(lo, width):
        return jnp.dot(xn, w_ref[:, lo:lo + width], preferred_element_type=F32)

    cos, sa, sb = cos_ref[...], sa_ref[...], sb_ref[...]

    def rope(z):
        return z * cos + pltpu.roll(z, LANES - ROT_DIM // 2, 1) * sa + pltpu.roll(z, ROT_DIM // 2, 1) * sb

    u_ref[...] = seg(0, POOL_WIDTH)
    zq = seg(O_Q, Q_WIDTH)
    for i in range(Q_WIDTH // LANES):
        sl = slice(i * LANES, (i + 1) * LANES)
        q_ref[:, sl] = (rope(zq[:, sl]) * Q_SCALE).astype(BF16)
    zk = seg(O_K, K_WIDTH)
    for i in range(K_WIDTH // LANES):
        sl = slice(i * LANES, (i + 1) * LANES)
        kr = rope(zk[:, sl])
        if k_transposed:
            kf_ref[sl, :] = kr.T
        else:
            kf_ref[:, sl] = kr
        kb_ref[:, sl] = kr.astype(BF16)
    zv = seg(O_V, V_WIDTH)
    for h in range(N_KV_HEADS):
        vf_ref[pl.ds(h, tm, stride=N_KV_HEADS), :] = zv[:, h * V_DIM:(h + 1) * V_DIM]
        vb_ref[:, 2 * h * V_DIM:(2 * h + 1) * V_DIM] = zv[:, h * V_DIM:(h + 1) * V_DIM].astype(BF16)
        vb_ref[:, (2 * h + 1) * V_DIM:(2 * h + 2) * V_DIM] = jnp.ones((tm, V_DIM), BF16)
    gate_ref[...] = seg(O_G, 2 * D_MODEL).astype(BF16)


def _proj(x3d, g, w_bf, cos, sa, sb, tm, k_transposed):
    b, t, _ = x3d.shape
    row = lambda bi, i: (bi, i, 0)
    tab = lambda bi, i: (i, 0)

    def rows(w, dt):
        return (pl.BlockSpec((None, tm, w), row), jax.ShapeDtypeStruct((b, t, w), dt))

    if k_transposed:
        k_out = (pl.BlockSpec((None, K_WIDTH, tm), lambda bi, i: (bi, 0, i)),
                 jax.ShapeDtypeStruct((b, K_WIDTH, t), F32))
    else:
        k_out = rows(K_WIDTH, F32)
    v_out = (pl.BlockSpec((None, tm * N_KV_HEADS, V_DIM), row),
             jax.ShapeDtypeStruct((b, t * N_KV_HEADS, V_DIM), F32))
    outs = [rows(POOL_WIDTH, F32), rows(Q_WIDTH, BF16), k_out, rows(K_WIDTH, BF16),
            v_out, rows(2 * V_WIDTH, BF16), rows(2 * D_MODEL, BF16)]
    return pl.pallas_call(
        functools.partial(_proj_kernel, k_transposed=k_transposed),
        grid=(b, t // tm),
        in_specs=[pl.BlockSpec((None, tm, D_MODEL), row), _const_spec((1, D_MODEL)),
                  _const_spec((D_MODEL, IN_WIDTH)),
                  pl.BlockSpec((tm, LANES), tab), pl.BlockSpec((tm, LANES), tab),
                  pl.BlockSpec((tm, LANES), tab)],
        out_specs=[o[0] for o in outs],
        out_shape=[o[1] for o in outs],
        compiler_params=_params(2),
        name="proj",
    )(x3d, g, w_bf, cos, sa, sb)


def _rope_tables(pos):
    half = ROT_DIM // 2
    j = jnp.arange(LANES, dtype=jnp.int32) % HEAD_DIM
    inv = ROPE_THETA ** (-(j % half).astype(F32) * 2.0 / ROT_DIM)
    ang = pos.astype(F32)[:, None] * inv[None, :]
    c, s = jnp.cos(ang), jnp.sin(ang)
    cos = jnp.where(j < ROT_DIM, c, 1.0)
    sa = jnp.where(j < half, -s, 0.0)
    sb = jnp.where((j >= half) & (j < ROT_DIM), s, 0.0)
    return cos, sa, sb


def _causal_pool(u_ref, hist_ref, ext_ref, tile_in_seq):
    tm = u_ref.shape[0]
    u = u_ref[...]
    ext_ref[0:HIST_ROWS, :] = jnp.where(tile_in_seq == 0, 0.0, hist_ref[...])
    ext_ref[HIST_ROWS:, :] = u
    pos = tile_in_seq * tm + lax.broadcasted_iota(jnp.int32, (tm, 1), 0)
    groups = []
    for g, w in enumerate(POOL_WINDOWS):
        sl = slice(g * POOL_GROUP_WIDTH, (g + 1) * POOL_GROUP_WIDTH)
        tot = ext_ref[:, sl]
        k = 1
        while k < w:
            tot = tot + pltpu.roll(tot, k, 0)
            k *= 2
        cnt = jnp.minimum(pos + 1, w).astype(F32)
        groups.append(tot[HIST_ROWS:] / cnt - u[:, sl])
    return groups


def _pool_sample_kernel(sp_ref, u_ref, o_ref, st_ref, *, past, n_new):
    def ext(r, sl):
        if r < POOL_HIST:
            return sp_ref[:, r * POOL_WIDTH + sl.start: r * POOL_WIDTH + sl.stop]
        r -= POOL_HIST
        return u_ref[:, r * POOL_WIDTH + sl.start: r * POOL_WIDTH + sl.stop]

    for t in range(n_new):
        for g, w in enumerate(POOL_WINDOWS):
            sl = slice(g * POOL_GROUP_WIDTH, (g + 1) * POOL_GROUP_WIDTH)
            cur = ext(POOL_HIST + t, sl)
            tot = cur
            for k in range(1, w):
                tot = tot + ext(POOL_HIST + t - k, sl)
            cnt = float(min(past + t + 1, w))
            o_ref[:, t * POOL_WIDTH + sl.start: t * POOL_WIDTH + sl.stop] = tot / cnt - cur
    keep = (POOL_HIST - n_new) * POOL_WIDTH
    st_ref[:, :keep] = sp_ref[:, n_new * POOL_WIDTH:]
    st_ref[:, keep:] = u_ref[...]


def _pool_sample(sp2, u2, past, n_new):
    return pl.pallas_call(
        functools.partial(_pool_sample_kernel, past=past, n_new=n_new),
        out_shape=[jax.ShapeDtypeStruct(u2.shape, F32), jax.ShapeDtypeStruct(sp2.shape, F32)],
        compiler_params=pltpu.CompilerParams(vmem_limit_bytes=VMEM_LIMIT),
        name="pool_sample",
    )(sp2, u2)


def _lambda(lp_ref, lam_init):
    lp = lp_ref[...]
    a = jnp.sum(lp[0:1] * lp[1:2], axis=-1, keepdims=True)
    b = jnp.sum(lp[2:3] * lp[3:4], axis=-1, keepdims=True)
    return jnp.exp(a) - jnp.exp(b) + lam_init


def _sample_tile_update(s, v_all, m_ref, l_ref, acc_ref, hrows):
    m_prev = m_ref[...]
    m_next = jnp.maximum(m_prev, jnp.max(s, axis=1, keepdims=True))
    p = jnp.exp2(s - _lane_tile(m_next, s.shape[1] // LANES))
    alpha = jnp.exp2(m_prev - m_next)
    l_ref[...] = alpha * l_ref[...] + jnp.sum(p, axis=1, keepdims=True)
    pv = jnp.dot(p.astype(BF16), v_all, preferred_element_type=F32)
    for h in range(N_KV_HEADS):
        hs = slice(h * hrows, (h + 1) * hrows)
        acc_ref[hs, :] = alpha[hs] * acc_ref[hs, :] + pv[hs, h * V_DIM:(h + 1) * V_DIM]
    m_ref[...] = m_next


def _attn_kernel(pt_ref, lp_ref, q_ref, k_ref, v_ref, sq_ref, kn_ref, vn_ref, *rest,
                 tq, tk, n_pages_step, steps_per_seq, n_new, lam_init):
    (kt_hbm, v_hbm, o_ref, so_ref, qs_ref, s_ref, m_ref, acc_ref,
     qall_ref, new_ref, sm_ref, sl_ref, sacc_ref, kbuf, vbuf, sem) = rest
    qi = pl.program_id(2)
    flat = (pl.program_id(0) * N_KV_HEADS + pl.program_id(1)) * pl.num_programs(2) + qi
    n_steps = pl.num_programs(0) * N_KV_HEADS * pl.num_programs(2)
    grp = lax.rem(flat, steps_per_seq)
    slot = lax.rem(flat, 2)

    def page_copies(step, to_slot):
        copies = []
        for n in range(n_pages_step):
            page = pt_ref[step * n_pages_step + n]
            copies.append(pltpu.make_async_copy(kt_hbm.at[page], kbuf.at[to_slot, n],
                                                sem.at[0, to_slot]))
            copies.append(pltpu.make_async_copy(v_hbm.at[page], vbuf.at[to_slot, n],
                                                sem.at[1, to_slot]))
        return copies

    @pl.when(flat == 0)
    def _():
        for c in page_copies(0, 0):
            c.start()

    @pl.when(flat + 1 < n_steps)
    def _():
        for c in page_copies(flat + 1, 1 - slot):
            c.start()

    for c in page_copies(flat, slot):
        c.wait()
    half = GROUP * n_new
    hrows = 2 * half
    srows = N_KV_HEADS * hrows

    @pl.when(grp == 0)
    def _():
        qall_ref[...] = jnp.zeros(qall_ref.shape, F32)
        lane = lax.broadcasted_iota(jnp.int32, (half, LANES), 1)
        zero = jnp.zeros((half, LANES), F32)
        for h in range(N_KV_HEADS):
            qh = sq_ref[h]
            r0 = h * hrows
            qall_ref[r0:r0 + half, h * LANES:(h + 1) * LANES] = jnp.where(lane < HEAD_DIM, qh, zero)
            qall_ref[r0 + half:r0 + hrows, h * LANES:(h + 1) * LANES] = \
                jnp.where(lane >= HEAD_DIM, qh, zero)
        sm_ref[...] = jnp.full(sm_ref.shape, NEG, F32)
        sl_ref[...] = jnp.zeros(sl_ref.shape, F32)
        sacc_ref[...] = jnp.zeros(sacc_ref.shape, F32)
        new_ref[...] = jnp.zeros(new_ref.shape, F32)
        new_ref[0, 0:n_new, :] = kn_ref[...]
        new_ref[1, 0:n_new, :] = vn_ref[...]
        s = lax.dot_general(qall_ref[...].astype(BF16), new_ref[0].astype(BF16),
                            (((1,), (1,)), ((), ())), preferred_element_type=F32)
        t_row = lax.rem(lax.broadcasted_iota(jnp.int32, (srows, LANES), 0), n_new)
        t_key = lax.broadcasted_iota(jnp.int32, (srows, LANES), 1)
        s = jnp.where(t_key <= t_row, s, NEG)
        _sample_tile_update(s, new_ref[1].astype(BF16), sm_ref, sl_ref, sacc_ref, hrows)

    rows = 2 * GROUP * tq
    lane = lax.broadcasted_iota(jnp.int32, (tq, LANES), 1)
    zero = jnp.zeros((tq, LANES), BF16)
    for g in range(GROUP):
        qg = q_ref[:, g * LANES:(g + 1) * LANES]
        qs_ref[(2 * g) * tq:(2 * g + 1) * tq, :] = jnp.where(lane < HEAD_DIM, qg, zero)
        qs_ref[(2 * g + 1) * tq:(2 * g + 2) * tq, :] = jnp.where(lane >= HEAD_DIM, qg, zero)
    m_ref[...] = jnp.full((rows, LANES), NEG, F32)
    acc_ref[...] = jnp.zeros((rows, 2 * V_DIM), F32)

    def scores(j, slot, width=tk):
        start = pl.multiple_of(j * tk, tk)
        s_ref[slot, :, :width] = lax.dot_general(
            qs_ref[...], k_ref[pl.ds(start, width), :], (((1,), (1,)), ((), ())),
            preferred_element_type=F32)

    def accumulate(j, slot, masked, width=tk):
        start = pl.multiple_of(j * tk, tk)
        s = s_ref[slot, :, :width]
        if masked:
            r = lax.broadcasted_iota(jnp.int32, (rows, width), 0)
            qpos = qi * tq + lax.rem(r, tq)
            kpos = start + lax.broadcasted_iota(jnp.int32, (rows, width), 1)
            s = jnp.where(kpos <= qpos, s, NEG)
        m_prev = m_ref[...]
        m_next = jnp.maximum(m_prev, jnp.max(s, axis=1, keepdims=True))
        p = jnp.exp2(s - _lane_tile(m_next, width // LANES)).astype(BF16)
        alpha = jnp.exp2(m_prev - m_next)
        pv = jnp.dot(p, v_ref[pl.ds(start, width), :], preferred_element_type=F32)
        acc_ref[...] = _lane_tile(alpha, 2) * acc_ref[...] + pv
        m_ref[...] = m_next

    n = ((qi + 1) * tq + tk - 1) // tk
    n_pairs = (n - 1) // 2
    scores(0, 0)

    qall = qall_ref[...].astype(BF16)
    k_all = jnp.concatenate([kbuf[slot, n].astype(BF16) for n in range(n_pages_step)], axis=1)
    s_pages = jnp.dot(qall, k_all, preferred_element_type=F32)
    v_all = jnp.concatenate(
        [jnp.concatenate([vbuf[slot, n, pl.ds(h, PAGE_SIZE, stride=N_KV_HEADS), :].astype(BF16)
                          for h in range(N_KV_HEADS)], axis=1)
         for n in range(n_pages_step)], axis=0)
    _sample_tile_update(s_pages, v_all, sm_ref, sl_ref, sacc_ref, hrows)

    @pl.when(grp == steps_per_seq - 1)
    def _():
        lam_s = _lambda(lp_ref, lam_init)
        o_s = sacc_ref[...] / sl_ref[...]
        for h in range(N_KV_HEADS):
            r0 = h * hrows
            so_ref[h] = o_s[r0:r0 + half] - lam_s * o_s[r0 + half:r0 + hrows]

    def pair_body(jj, c):
        j = 2 * jj
        scores(j + 1, 1)
        accumulate(j, 0, False)
        scores(j + 2, 0)
        accumulate(j + 1, 1, False)
        return c

    def quad_body(qq, c):
        pair_body(2 * qq, c)
        return pair_body(2 * qq + 1, c)

    n_quads = n_pairs // 2
    lax.fori_loop(0, n_quads, quad_body, 0)
    lax.fori_loop(2 * n_quads, n_pairs, pair_body, 0)
    odd_tail = n - 2 * n_pairs == 1
    sub = lax.rem(qi, tk // tq)
    for w in range(tk // tq):
        width = (w + 1) * tq

        @pl.when(jnp.logical_and(odd_tail, sub == w))
        def _():
            accumulate(n - 1, 0, True, width)

        @pl.when(jnp.logical_and(jnp.logical_not(odd_tail), sub == w))
        def _():
            scores(n - 1, 1, width)
            accumulate(n - 2, 0, False)
            accumulate(n - 1, 1, True, width)

    lam = _lambda(lp_ref, lam_init)
    acc = acc_ref[...]
    o = acc[:, :V_DIM] / acc[:, V_DIM:]
    for g in range(GROUP):
        o0 = o[(2 * g) * tq:(2 * g + 1) * tq]
        o1 = o[(2 * g + 1) * tq:(2 * g + 2) * tq]
        o_ref[:, g * V_DIM:(g + 1) * V_DIM] = o0 - lam * o1


def _attention(page_table, lp, q3, k3, v3, sq4, kn3, vn3, kt_pages, v_pages, lam_init, tq, tk):
    b, s, _ = q3.shape
    n_seq, n_pages = page_table.shape
    n_new = kn3.shape[1]
    nq = s // tq
    n_steps = b * N_KV_HEADS * nq
    steps_per_seq, rem = divmod(n_steps, n_seq)
    assert rem == 0 and n_pages % steps_per_seq == 0, (n_steps, n_seq, n_pages)
    n_pages_step = n_pages // steps_per_seq
    rows = 2 * GROUP * tq
    srows = N_KV_HEADS * 2 * GROUP * n_new
    qw = GROUP * 2 * HEAD_DIM
    once = pl.Buffered(1)

    def seq_of(bi, h, i):
        return ((bi * N_KV_HEADS + h) * nq + i) // steps_per_seq

    step_pages = page_table.reshape(n_steps * n_pages_step)
    page_buf = pltpu.VMEM((2, n_pages_step) + kt_pages.shape[1:], F32)

    sample_block = lambda bi, h, i, pt: (seq_of(bi, h, i), 0, 0, 0)
    new_block = lambda bi, h, i, pt: (seq_of(bi, h, i), 0, 0)
    grid_spec = pltpu.PrefetchScalarGridSpec(
        num_scalar_prefetch=1,
        grid=(b, N_KV_HEADS, nq),
        in_specs=[pl.BlockSpec(lp.shape, lambda bi, h, i, pt: (0, 0)),
                  pl.BlockSpec((None, tq, qw), lambda bi, h, i, pt: (bi, i, h)),
                  pl.BlockSpec((None, s, LANES), lambda bi, h, i, pt: (bi, 0, h),
                               pipeline_mode=once),
                  pl.BlockSpec((None, s, 2 * V_DIM), lambda bi, h, i, pt: (bi, 0, h),
                               pipeline_mode=once),
                  pl.BlockSpec((None,) + sq4.shape[1:], sample_block),
                  pl.BlockSpec((None, n_new, K_WIDTH), new_block),
                  pl.BlockSpec((None, n_new, V_WIDTH), new_block),
                  pl.BlockSpec(memory_space=pl.ANY), pl.BlockSpec(memory_space=pl.ANY)],
        out_specs=[pl.BlockSpec((None, tq, GROUP * V_DIM), lambda bi, h, i, pt: (bi, i, h)),
                   pl.BlockSpec((None,) + sq4.shape[1:], sample_block)],
        scratch_shapes=[pltpu.VMEM((rows, LANES), BF16), pltpu.VMEM((2, rows, tk), F32),
                        pltpu.VMEM((rows, LANES), F32), pltpu.VMEM((rows, 2 * V_DIM), F32),
                        pltpu.VMEM((srows, K_WIDTH), F32),
                        pltpu.VMEM((2, PAGE_SIZE, K_WIDTH), F32),
                        pltpu.VMEM((srows, LANES), F32), pltpu.VMEM((srows, LANES), F32),
                        pltpu.VMEM((srows, V_DIM), F32),
                        page_buf, page_buf, pltpu.SemaphoreType.DMA((2, 2))])
    return pl.pallas_call(
        functools.partial(_attn_kernel, tq=tq, tk=tk, n_pages_step=n_pages_step,
                          steps_per_seq=steps_per_seq, n_new=n_new, lam_init=lam_init),
        grid_spec=grid_spec,
        out_shape=[jax.ShapeDtypeStruct((b, s, ATTN_WIDTH), F32),
                   jax.ShapeDtypeStruct(sq4.shape, F32)],
        compiler_params=_params(3),
        name="attention",
    )(step_pages, lp, q3, k3, v3, sq4, kn3, vn3, kt_pages, v_pages)


def _merge_kernel(x_ref, *refs, lam_init, tiles_per_seq):
    if tiles_per_seq:
        (u_ref, hist_ref, attn_ref, gate_ref, wpg_ref, ps_ref, sg_ref,
         wpp_ref, wap_ref, wo_ref, o_ref, ext_ref) = refs
        pooled = _causal_pool(u_ref, hist_ref, ext_ref,
                              lax.rem(pl.program_id(0), tiles_per_seq))
    else:
        (pooled_ref, attn_ref, gate_ref, wpg_ref, ps_ref, sg_ref,
         wpp_ref, wap_ref, wo_ref, o_ref) = refs
        pooled = [pooled_ref[:, g * POOL_GROUP_WIDTH:(g + 1) * POOL_GROUP_WIDTH]
                  for g in range(N_POOL_GROUPS)]
    mixed = [jnp.dot(pooled[g].astype(BF16), wpg_ref[g], preferred_element_type=F32)
             for g in range(N_POOL_GROUPS)]
    pool_out = jnp.concatenate(mixed, axis=1) * ps_ref[...]
    a_branch = jnp.dot(pool_out.astype(BF16), wpp_ref[...], preferred_element_type=F32)
    sg = sg_ref[...]
    heads = []
    for h in range(N_HEADS):
        oh = attn_ref[:, h * V_DIM:(h + 1) * V_DIM]
        heads.append((_rms(oh, sg) * (1.0 - lam_init)).astype(BF16))
    b_branch = jnp.dot(jnp.concatenate(heads, axis=1), wap_ref[...], preferred_element_type=F32)
    ga = gate_ref[:, :D_MODEL].astype(F32)
    gb = gate_ref[:, D_MODEL:].astype(F32)
    mix = jax.nn.sigmoid(ga) * a_branch + jax.nn.sigmoid(gb) * b_branch
    o_ref[...] = x_ref[...] + jnp.dot(mix.astype(BF16), wo_ref[...], preferred_element_type=F32)


def _merge(x2d, pool_in, attn, gates, wpg, ps, sg, wpp, wap, wo, lam_init, tm, seq_len=None):
    n = x2d.shape[0]
    row = lambda i: (i, 0)
    pool_specs = [pl.BlockSpec((tm, POOL_WIDTH), row)]
    pool_args, scratch, tiles_per_seq = [pool_in], [], 0
    if seq_len is not None:
        per = tm // HIST_ROWS
        pool_specs.append(pl.BlockSpec((HIST_ROWS, POOL_WIDTH),
                                       lambda i: (jnp.maximum(i * per - 1, 0), 0)))
        pool_args, tiles_per_seq = [pool_in, pool_in], seq_len // tm
        scratch = [pltpu.VMEM((HIST_ROWS + tm, POOL_WIDTH), F32)]
    return pl.pallas_call(
        functools.partial(_merge_kernel, lam_init=lam_init, tiles_per_seq=tiles_per_seq),
        grid=(n // tm,),
        in_specs=[pl.BlockSpec((tm, D_MODEL), row)] + pool_specs
                 + [pl.BlockSpec((tm, ATTN_WIDTH), row), pl.BlockSpec((tm, 2 * D_MODEL), row),
                    _const_spec(wpg.shape), _const_spec(ps.shape), _const_spec(sg.shape),
                    _const_spec(wpp.shape), _const_spec(wap.shape), _const_spec(wo.shape)],
        out_specs=pl.BlockSpec((tm, D_MODEL), row),
        out_shape=jax.ShapeDtypeStruct((n, D_MODEL), F32),
        scratch_shapes=scratch,
        compiler_params=_params(1),
        name="merge",
    )(x2d, *pool_args, attn, gates, wpg, ps, sg, wpp, wap, wo)


def _ffn_kernel(x_ref, nf_ref, wg_ref, wu_ref, wd_ref, fn_ref, o_ref):
    x = x_ref[...]
    h = _rms(x, nf_ref[...]).astype(BF16)
    gate = jnp.dot(h, wg_ref[...], preferred_element_type=F32)
    up = jnp.dot(h, wu_ref[...], preferred_element_type=F32)
    act = (jax.nn.silu(gate) * up).astype(BF16)
    y = x + jnp.dot(act, wd_ref[...], preferred_element_type=F32)
    o_ref[...] = _rms(y, fn_ref[...])


def _ffn(x2d, nf, wg, wu, wd, fn, tm):
    n = x2d.shape[0]
    row = lambda i: (i, 0)
    return pl.pallas_call(
        _ffn_kernel,
        grid=(n // tm,),
        in_specs=[pl.BlockSpec((tm, D_MODEL), row), _const_spec(nf.shape), _const_spec(wg.shape),
                  _const_spec(wu.shape), _const_spec(wd.shape), _const_spec(fn.shape)],
        out_specs=pl.BlockSpec((tm, D_MODEL), row),
        out_shape=jax.ShapeDtypeStruct((n, D_MODEL), F32),
        compiler_params=_params(1),
        name="ffn",
    )(x2d, nf, wg, wu, wd, fn)


def kernel(x_prompt, x_sample, cache_k, cache_v, state_pool, page_table, norm_mix, w_in,
           w_pool_group, pool_scale, lambda_q1, lambda_k1, lambda_q2, lambda_k2, subln_g,
           w_pool_proj, w_attn_proj, w_out, norm_ffn, w_gate, w_up, w_down, final_norm):
    depth = w_in.shape[0]
    assert depth == 1, "single-layer step"
    batch, seq, _ = x_prompt.shape
    dec_batch, dec_seq, _ = x_sample.shape
    n_pages = page_table.shape[1]
    past = n_pages * PAGE_SIZE
    n_pool = cache_k.shape[1]
    l = 0
    lam_init = 0.8 - 0.6 * math.exp(-0.3 * l)

    tm = 512
    row2 = lambda a: a.reshape(1, -1)
    w_in_bf = w_in[l].astype(BF16)
    wpg = w_pool_group[l].astype(BF16)
    wpp = w_pool_proj[l].astype(BF16)
    wap = w_attn_proj[l].astype(BF16)
    wo = w_out[l].astype(BF16)
    wg = w_gate[l].astype(BF16)
    wu = w_up[l].astype(BF16)
    wd = w_down[l].astype(BF16)
    g_mix, ps, sg = row2(norm_mix[l]), row2(pool_scale[l]), row2(subln_g[l])
    nf, fn = row2(norm_ffn[l]), row2(final_norm)
    lp = jnp.stack([lambda_q1[l], lambda_k1[l], lambda_q2[l], lambda_k2[l]]).astype(F32)

    def tail(x2d, pool_in, attn, gates, tile, seq_len=None):
        x1 = _merge(x2d, pool_in, attn, gates, wpg, ps, sg, wpp, wap, wo, lam_init, tile, seq_len)
        return _ffn(x1, nf, wg, wu, wd, fn, tile)

    n_p = batch * seq
    n_s = dec_batch * dec_seq
    tabs_p = _rope_tables(jnp.arange(seq, dtype=jnp.int32))
    tabs_s = [jnp.tile(t, (dec_batch, 1))
              for t in _rope_tables(past + jnp.arange(dec_seq, dtype=jnp.int32))]
    u3, q_p, kt, kb, vf_p, vb, gates_p = _proj(x_prompt, g_mix, w_in_bf, *tabs_p, tm, True)
    u_s, q_s, kf_s, _, vf_s, _, gates_s = _proj(x_sample.reshape(1, n_s, D_MODEL), g_mix, w_in_bf,
                                                *tabs_s, n_s, False)

    q4 = q_s.reshape(dec_batch, dec_seq, N_KV_HEADS, GROUP, V_DIM).transpose(0, 2, 3, 1, 4)
    q4 = q4.reshape(dec_batch, N_KV_HEADS, GROUP * dec_seq, V_DIM).astype(F32)
    kt_pages = cache_k.transpose(0, 1, 3, 4, 5, 2).reshape(n_pool, K_WIDTH, PAGE_SIZE)
    v_pages = cache_v.reshape(n_pool, PAGE_SIZE * N_KV_HEADS, V_DIM)
    attn_p, o4 = _attention(page_table, lp, q_p, kb, vb, q4,
                            kf_s.reshape(dec_batch, dec_seq, K_WIDTH),
                            vf_s.reshape(dec_batch, dec_seq, V_WIDTH),
                            kt_pages, v_pages, lam_init, 256, 1024)
    attn_s = o4.reshape(dec_batch, N_KV_HEADS, GROUP, dec_seq, V_DIM).transpose(0, 3, 1, 2, 4)

    y_prompt = tail(x_prompt.reshape(n_p, D_MODEL), u3.reshape(n_p, POOL_WIDTH),
                    attn_p.reshape(n_p, ATTN_WIDTH), gates_p.reshape(n_p, 2 * D_MODEL), tm,
                    seq).reshape(x_prompt.shape)
    k_prompt = kt.reshape(1, batch, N_KV_HEADS, 2, HEAD_DIM, seq).transpose(0, 1, 5, 2, 3, 4)
    v_prompt = vf_p.reshape(1, batch, seq, N_KV_HEADS, V_DIM)
    pool_prompt = u3[:, seq - POOL_HIST:, :][None]

    pooled_s, pool_state = _pool_sample(
        state_pool[l].reshape(dec_batch, POOL_HIST * POOL_WIDTH),
        u_s.reshape(dec_batch, dec_seq * POOL_WIDTH), past, dec_seq)
    y_sample = tail(x_sample.reshape(n_s, D_MODEL), pooled_s.reshape(n_s, POOL_WIDTH),
                    attn_s.reshape(n_s, ATTN_WIDTH), gates_s.reshape(n_s, 2 * D_MODEL),
                    n_s).reshape(x_sample.shape)
    k_sample = kf_s.reshape(1, dec_batch, dec_seq, N_KV_HEADS, 2, HEAD_DIM)
    v_sample = vf_s.reshape(1, dec_batch, dec_seq, N_KV_HEADS, V_DIM)
    pool_sample = pool_state.reshape(1, dec_batch, POOL_HIST, POOL_WIDTH)

    return (y_prompt, y_sample, k_prompt, v_prompt, pool_prompt,
            k_sample, v_sample, pool_sample)
```

```python
import functools
import math

import jax
import jax.numpy as jnp
from jax import lax
from jax.experimental import pallas as pl
from jax.experimental.pallas import tpu as pltpu

D_MODEL = 1024
PAGE_SIZE = 128
POOL_WINDOWS = (2, 4, 8, 16)
N_POOL_GROUPS = len(POOL_WINDOWS)
POOL_GROUP_WIDTH = D_MODEL // 8
POOL_WIDTH = N_POOL_GROUPS * POOL_GROUP_WIDTH
POOL_HIST = max(POOL_WINDOWS) - 1
N_HEADS = 8
N_KV_HEADS = 4
GROUP = N_HEADS // N_KV_HEADS
HEAD_DIM = D_MODEL // N_HEADS // 2
V_DIM = 2 * HEAD_DIM
ROT_DIM = HEAD_DIM // 4
ROPE_THETA = 500000.0
Q_WIDTH = N_HEADS * 2 * HEAD_DIM
K_WIDTH = N_KV_HEADS * 2 * HEAD_DIM
V_WIDTH = N_KV_HEADS * V_DIM
ATTN_WIDTH = N_HEADS * V_DIM
D_FF = ((8 * D_MODEL + 3 * 256 - 1) // (3 * 256)) * 256
EPS = 1e-6
NEG = -1e30
Q_SCALE = HEAD_DIM ** -0.5 * math.log2(math.e)

LANES = 128
HIST_ROWS = 16
VMEM_LIMIT = 56 * 1024 * 1024
ROW_TILE = 512
ATTN_Q_TILE = 256
ATTN_K_TILE = 1024

F32 = jnp.float32
BF16 = jnp.bfloat16

O_Q = POOL_WIDTH
O_K = O_Q + Q_WIDTH
O_V = O_K + K_WIDTH
O_G = O_V + V_WIDTH
IN_WIDTH = O_G + 2 * D_MODEL


def _const_spec(shape):
    zeros = (0,) * len(shape)
    return pl.BlockSpec(shape, lambda *_: zeros, pipeline_mode=pl.Buffered(1))


def _params(n_grid):
    return pltpu.CompilerParams(
        dimension_semantics=("arbitrary",) * n_grid, vmem_limit_bytes=VMEM_LIMIT)


def _lane_tile(x, n):
    return x if n == 1 else jnp.concatenate([x] * n, axis=1)


def _rms(x, g):
    return x * lax.rsqrt(jnp.mean(x * x, axis=-1, keepdims=True) + EPS) * g


def _proj_kernel(x_ref, g_ref, w_ref, cos_ref, sa_ref, sb_ref,
                 u_ref, q_ref, kf_ref, kb_ref, vf_ref, vb_ref, gate_ref, *, k_transposed):
    tm = x_ref.shape[0]
    xn = _rms(x_ref[...], g_ref[...]).astype(BF16)

    def seg(lo, width):
        return jnp.dot(xn, w_ref[:, lo:lo + width], preferred_element_type=F32)

    cos, sa, sb = cos_ref[...], sa_ref[...], sb_ref[...]

    def rope(z):
        return z * cos + pltpu.roll(z, LANES - ROT_DIM // 2, 1) * sa + pltpu.roll(z, ROT_DIM // 2, 1) * sb

    u_ref[...] = seg(0, POOL_WIDTH)
    zq = seg(O_Q, Q_WIDTH)
    for i in range(Q_WIDTH // LANES):
        sl = slice(i * LANES, (i + 1) * LANES)
        q_ref[:, sl] = (rope(zq[:, sl]) * Q_SCALE).astype(BF16)
    zk = seg(O_K, K_WIDTH)
    for i in range(K_WIDTH // LANES):
        sl = slice(i * LANES, (i + 1) * LANES)
        kr = rope(zk[:, sl])
        if k_transposed:
            kf_ref[sl, :] = kr.T
        else:
            kf_ref[:, sl] = kr
        kb_ref[:, sl] = kr.astype(BF16)
    zv = seg(O_V, V_WIDTH)
    for h in range(N_KV_HEADS):
        vf_ref[pl.ds(h, tm, stride=N_KV_HEADS), :] = zv[:, h * V_DIM:(h + 1) * V_DIM]
        vb_ref[:, 2 * h * V_DIM:(2 * h + 1) * V_DIM] = zv[:, h * V_DIM:(h + 1) * V_DIM].astype(BF16)
        vb_ref[:, (2 * h + 1) * V_DIM:(2 * h + 2) * V_DIM] = jnp.ones((tm, V_DIM), BF16)
    gate_ref[...] = seg(O_G, 2 * D_MODEL).astype(BF16)


def _proj(x3d, g, w_bf, cos, sa, sb, tm, k_transposed):
    b, t, _ = x3d.shape
    row = lambda bi, i: (bi, i, 0)
    tab = lambda bi, i: (i, 0)

    def rows(w, dt):
        return (pl.BlockSpec((None, tm, w), row), jax.ShapeDtypeStruct((b, t, w), dt))

    if k_transposed:
        k_out = (pl.BlockSpec((None, K_WIDTH, tm), lambda bi, i: (bi, 0, i)),
                 jax.ShapeDtypeStruct((b, K_WIDTH, t), F32))
    else:
        k_out = rows(K_WIDTH, F32)
    v_out = (pl.BlockSpec((None, tm * N_KV_HEADS, V_DIM), row),
             jax.ShapeDtypeStruct((b, t * N_KV_HEADS, V_DIM), F32))
    outs = [rows(POOL_WIDTH, F32), rows(Q_WIDTH, BF16), k_out, rows(K_WIDTH, BF16),
            v_out, rows(2 * V_WIDTH, BF16), rows(2 * D_MODEL, BF16)]
    return pl.pallas_call(
        functools.partial(_proj_kernel, k_transposed=k_transposed),
        grid=(b, t // tm),
        in_specs=[pl.BlockSpec((None, tm, D_MODEL), row), _const_spec((1, D_MODEL)),
                  _const_spec((D_MODEL, IN_WIDTH)),
                  pl.BlockSpec((tm, LANES), tab), pl.BlockSpec((tm, LANES), tab),
                  pl.BlockSpec((tm, LANES), tab)],
        out_specs=[o[0] for o in outs],
        out_shape=[o[1] for o in outs],
        compiler_params=_params(2),
        name="proj",
    )(x3d, g, w_bf, cos, sa, sb)


def _rope_tables(pos):
    half = ROT_DIM // 2
    j = jnp.arange(LANES, dtype=jnp.int32) % HEAD_DIM
    inv = ROPE_THETA ** (-(j % half).astype(F32) * 2.0 / ROT_DIM)
    ang = pos.astype(F32)[:, None] * inv[None, :]
    c, s = jnp.cos(ang), jnp.sin(ang)
    cos = jnp.where(j < ROT_DIM, c, 1.0)
    sa = jnp.where(j < half, -s, 0.0)
    sb = jnp.where((j >= half) & (j < ROT_DIM), s, 0.0)
    return cos, sa, sb


def _causal_pool(u_ref, hist_ref, ext_ref, tile_in_seq):
    tm = u_ref.shape[0]
    u = u_ref[...]
    ext_ref[0:HIST_ROWS, :] = jnp.where(tile_in_seq == 0, 0.0, hist_ref[...])
    ext_ref[HIST_ROWS:, :] = u
    pos = tile_in_seq * tm + lax.broadcasted_iota(jnp.int32, (tm, 1), 0)
    groups = []
    for g, w in enumerate(POOL_WINDOWS):
        sl = slice(g * POOL_GROUP_WIDTH, (g + 1) * POOL_GROUP_WIDTH)
        tot = ext_ref[:, sl]
        k = 1
        while k < w:
            tot = tot + pltpu.roll(tot, k, 0)
            k *= 2
        cnt = jnp.minimum(pos + 1, w).astype(F32)
        groups.append(tot[HIST_ROWS:] / cnt - u[:, sl])
    return groups


def _pool_sample_kernel(sp_ref, u_ref, o_ref, st_ref, *, past, n_new):
    def ext(r, sl):
        if r < POOL_HIST:
            return sp_ref[:, r * POOL_WIDTH + sl.start: r * POOL_WIDTH + sl.stop]
        r -= POOL_HIST
        return u_ref[:, r * POOL_WIDTH + sl.start: r * POOL_WIDTH + sl.stop]

    for t in range(n_new):
        for g, w in enumerate(POOL_WINDOWS):
            sl = slice(g * POOL_GROUP_WIDTH, (g + 1) * POOL_GROUP_WIDTH)
            cur = ext(POOL_HIST + t, sl)
            tot = cur
            for k in range(1, w):
                tot = tot + ext(POOL_HIST + t - k, sl)
            cnt = float(min(past + t + 1, w))
            o_ref[:, t * POOL_WIDTH + sl.start: t * POOL_WIDTH + sl.stop] = tot / cnt - cur
    keep = (POOL_HIST - n_new) * POOL_WIDTH
    st_ref[:, :keep] = sp_ref[:, n_new * POOL_WIDTH:]
    st_ref[:, keep:] = u_ref[...]


def _pool_sample(sp2, u2, past, n_new):
    return pl.pallas_call(
        functools.partial(_pool_sample_kernel, past=past, n_new=n_new),
        out_shape=[jax.ShapeDtypeStruct(u2.shape, F32), jax.ShapeDtypeStruct(sp2.shape, F32)],
        compiler_params=pltpu.CompilerParams(vmem_limit_bytes=VMEM_LIMIT),
        name="pool_sample",
    )(sp2, u2)


def _lambda(lp_ref, lam_init):
    lp = lp_ref[...]
    a = jnp.sum(lp[0:1] * lp[1:2], axis=-1, keepdims=True)
    b = jnp.sum(lp[2:3] * lp[3:4], axis=-1, keepdims=True)
    return jnp.exp(a) - jnp.exp(b) + lam_init


def _sample_tile_update(s, v_all, m_ref, l_ref, acc_ref, hrows):
    m_prev = m_ref[...]
    m_next = jnp.maximum(m_prev, jnp.max(s, axis=1, keepdims=True))
    p = jnp.exp2(s - _lane_tile(m_next, s.shape[1] // LANES))
    alpha = jnp.exp2(m_prev - m_next)
    l_ref[...] = alpha * l_ref[...] + jnp.sum(p, axis=1, keepdims=True)
    pv = jnp.dot(p.astype(BF16), v_all, preferred_element_type=F32)
    for h in range(N_KV_HEADS):
        hs = slice(h * hrows, (h + 1) * hrows)
        acc_ref[hs, :] = alpha[hs] * acc_ref[hs, :] + pv[hs, h * V_DIM:(h + 1) * V_DIM]
    m_ref[...] = m_next


def _attn_kernel(pt_ref, lp_ref, q_ref, k_ref, v_ref, sq_ref, kn_ref, vn_ref, *rest,
                 tq, tk, n_pages_step, steps_per_seq, n_new, lam_init):
    (kt_hbm, v_hbm, o_ref, so_ref, qs_ref, s_ref, m_ref, acc_ref,
     qall_ref, new_ref, sm_ref, sl_ref, sacc_ref, kbuf, vbuf, sem) = rest
    qi = pl.program_id(2)
    flat = (pl.program_id(0) * N_KV_HEADS + pl.program_id(1)) * pl.num_programs(2) + qi
    n_steps = pl.num_programs(0) * N_KV_HEADS * pl.num_programs(2)
    grp = lax.rem(flat, steps_per_seq)
    slot = lax.rem(flat, 2)

    def page_copies(step, to_slot):
        copies = []
        for n in range(n_pages_step):
            page = pt_ref[step * n_pages_step + n]
            copies.append(pltpu.make_async_copy(kt_hbm.at[page], kbuf.at[to_slot, n],
                                                sem.at[0, to_slot]))
            copies.append(pltpu.make_async_copy(v_hbm.at[page], vbuf.at[to_slot, n],
                                                sem.at[1, to_slot]))
        return copies

    @pl.when(flat == 0)
    def _():
        for c in page_copies(0, 0):
            c.start()

    @pl.when(flat + 1 < n_steps)
    def _():
        for c in page_copies(flat + 1, 1 - slot):
            c.start()

    for c in page_copies(flat, slot):
        c.wait()
    half = GROUP * n_new
    hrows = 2 * half
    srows = N_KV_HEADS * hrows

    @pl.when(grp == 0)
    def _():
        qall_ref[...] = jnp.zeros(qall_ref.shape, F32)
        lane = lax.broadcasted_iota(jnp.int32, (half, LANES), 1)
        zero = jnp.zeros((half, LANES), F32)
        for h in range(N_KV_HEADS):
            qh = sq_ref[h]
            r0 = h * hrows
            qall_ref[r0:r0 + half, h * LANES:(h + 1) * LANES] = jnp.where(lane < HEAD_DIM, qh, zero)
            qall_ref[r0 + half:r0 + hrows, h * LANES:(h + 1) * LANES] = \
                jnp.where(lane >= HEAD_DIM, qh, zero)
        sm_ref[...] = jnp.full(sm_ref.shape, NEG, F32)
        sl_ref[...] = jnp.zeros(sl_ref.shape, F32)
        sacc_ref[...] = jnp.zeros(sacc_ref.shape, F32)
        new_ref[...] = jnp.zeros(new_ref.shape, F32)
        new_ref[0, 0:n_new, :] = kn_ref[...]
        new_ref[1, 0:n_new, :] = vn_ref[...]
        s = lax.dot_general(qall_ref[...].astype(BF16), new_ref[0].astype(BF16),
                            (((1,), (1,)), ((), ())), preferred_element_type=F32)
        t_row = lax.rem(lax.broadcasted_iota(jnp.int32, (srows, LANES), 0), n_new)
        t_key = lax.broadcasted_iota(jnp.int32, (srows, LANES), 1)
        s = jnp.where(t_key <= t_row, s, NEG)
        _sample_tile_update(s, new_ref[1].astype(BF16), sm_ref, sl_ref, sacc_ref, hrows)

    rows = 2 * GROUP * tq
    lane = lax.broadcasted_iota(jnp.int32, (tq, LANES), 1)
    zero = jnp.zeros((tq, LANES), BF16)
    for g in range(GROUP):
        qg = q_ref[:, g * LANES:(g + 1) * LANES]
        qs_ref[(2 * g) * tq:(2 * g + 1) * tq, :] = jnp.where(lane < HEAD_DIM, qg, zero)
        qs_ref[(2 * g + 1) * tq:(2 * g + 2) * tq, :] = jnp.where(lane >= HEAD_DIM, qg, zero)
    m_ref[...] = jnp.full((rows, LANES), NEG, F32)
    acc_ref[...] = jnp.zeros((rows, 2 * V_DIM), F32)

    def scores(j, slot, width=tk):
        start = pl.multiple_of(j * tk, tk)
        s_ref[slot, :, :width] = lax.dot_general(
            qs_ref[...], k_ref[pl.ds(start, width), :], (((1,), (1,)), ((), ())),
            preferred_element_type=F32)

    def accumulate(j, slot, masked, width=tk):
        start = pl.multiple_of(j * tk, tk)
        s = s_ref[slot, :, :width]
        if masked:
            r = lax.broadcasted_iota(jnp.int32, (rows, width), 0)
            qpos = qi * tq + lax.rem(r, tq)
            kpos = start + lax.broadcasted_iota(jnp.int32, (rows, width), 1)
            s = jnp.where(kpos <= qpos, s, NEG)
        m_prev = m_ref[...]
        m_next = jnp.maximum(m_prev, jnp.max(s, axis=1, keepdims=True))
        p = jnp.exp2(s - _lane_tile(m_next, width // LANES)).astype(BF16)
        alpha = jnp.exp2(m_prev - m_next)
        pv = jnp.dot(p, v_ref[pl.ds(start, width), :], preferred_element_type=F32)
        acc_ref[...] = _lane_tile(alpha, 2) * acc_ref[...] + pv
        m_ref[...] = m_next

    n = ((qi + 1) * tq + tk - 1) // tk
    n_pairs = (n - 1) // 2
    scores(0, 0)

    qall = qall_ref[...].astype(BF16)
    k_all = jnp.concatenate([kbuf[slot, n].astype(BF16) for n in range(n_pages_step)], axis=1)
    s_pages = jnp.dot(qall, k_all, preferred_element_type=F32)
    v_all = jnp.concatenate(
        [jnp.concatenate([vbuf[slot, n, pl.ds(h, PAGE_SIZE, stride=N_KV_HEADS), :].astype(BF16)
                          for h in range(N_KV_HEADS)], axis=1)
         for n in range(n_pages_step)], axis=0)
    _sample_tile_update(s_pages, v_all, sm_ref, sl_ref, sacc_ref, hrows)

    @pl.when(grp == steps_per_seq - 1)
    def _():
        lam_s = _lambda(lp_ref, lam_init)
        o_s = sacc_ref[...] / sl_ref[...]
        for h in range(N_KV_HEADS):
            r0 = h * hrows
            so_ref[h] = o_s[r0:r0 + half] - lam_s * o_s[r0 + half:r0 + hrows]

    def pair_body(jj, c):
        j = 2 * jj
        scores(j + 1, 1)
        accumulate(j, 0, False)
        scores(j + 2, 0)
        accumulate(j + 1, 1, False)
        return c

    def quad_body(qq, c):
        pair_body(2 * qq, c)
        return pair_body(2 * qq + 1, c)

    def oct_body(oo, c):
        quad_body(2 * oo, c)
        return quad_body(2 * oo + 1, c)

    n_quads = n_pairs // 2
    n_octs = n_quads // 2
    lax.fori_loop(0, n_octs, oct_body, 0)
    lax.fori_loop(2 * n_octs, n_quads, quad_body, 0)
    lax.fori_loop(2 * n_quads, n_pairs, pair_body, 0)
    odd_tail = n - 2 * n_pairs == 1
    sub = lax.rem(qi, tk // tq)
    for w in range(tk // tq):
        width = (w + 1) * tq

        @pl.when(jnp.logical_and(odd_tail, sub == w))
        def _():
            accumulate(n - 1, 0, True, width)

        @pl.when(jnp.logical_and(jnp.logical_not(odd_tail), sub == w))
        def _():
            scores(n - 1, 1, width)
            accumulate(n - 2, 0, False)
            accumulate(n - 1, 1, True, width)

    lam = _lambda(lp_ref, lam_init)
    acc = acc_ref[...]
    o = acc[:, :V_DIM] / acc[:, V_DIM:]
    for g in range(GROUP):
        o0 = o[(2 * g) * tq:(2 * g + 1) * tq]
        o1 = o[(2 * g + 1) * tq:(2 * g + 2) * tq]
        o_ref[:, g * V_DIM:(g + 1) * V_DIM] = o0 - lam * o1


def _attention(page_table, lp, q3, k3, v3, sq4, kn3, vn3, kt_pages, v_pages, lam_init, tq, tk):
    b, s, _ = q3.shape
    n_seq, n_pages = page_table.shape
    n_new = kn3.shape[1]
    nq = s // tq
    n_steps = b * N_KV_HEADS * nq
    steps_per_seq, rem = divmod(n_steps, n_seq)
    assert rem == 0 and n_pages % steps_per_seq == 0, (n_steps, n_seq, n_pages)
    n_pages_step = n_pages // steps_per_seq
    rows = 2 * GROUP * tq
    srows = N_KV_HEADS * 2 * GROUP * n_new
    qw = GROUP * 2 * HEAD_DIM
    once = pl.Buffered(1)

    def seq_of(bi, h, i):
        return ((bi * N_KV_HEADS + h) * nq + i) // steps_per_seq

    step_pages = page_table.reshape(n_steps * n_pages_step)
    page_buf = pltpu.VMEM((2, n_pages_step) + kt_pages.shape[1:], F32)

    sample_block = lambda bi, h, i, pt: (seq_of(bi, h, i), 0, 0, 0)
    new_block = lambda bi, h, i, pt: (seq_of(bi, h, i), 0, 0)
    grid_spec = pltpu.PrefetchScalarGridSpec(
        num_scalar_prefetch=1,
        grid=(b, N_KV_HEADS, nq),
        in_specs=[pl.BlockSpec(lp.shape, lambda bi, h, i, pt: (0, 0)),
                  pl.BlockSpec((None, tq, qw), lambda bi, h, i, pt: (bi, i, h)),
                  pl.BlockSpec((None, s, LANES), lambda bi, h, i, pt: (bi, 0, h),
                               pipeline_mode=once),
                  pl.BlockSpec((None, s, 2 * V_DIM), lambda bi, h, i, pt: (bi, 0, h),
                               pipeline_mode=once),
                  pl.BlockSpec((None,) + sq4.shape[1:], sample_block),
                  pl.BlockSpec((None, n_new, K_WIDTH), new_block),
                  pl.BlockSpec((None, n_new, V_WIDTH), new_block),
                  pl.BlockSpec(memory_space=pl.ANY), pl.BlockSpec(memory_space=pl.ANY)],
        out_specs=[pl.BlockSpec((None, tq, GROUP * V_DIM), lambda bi, h, i, pt: (bi, i, h)),
                   pl.BlockSpec((None,) + sq4.shape[1:], sample_block)],
        scratch_shapes=[pltpu.VMEM((rows, LANES), BF16), pltpu.VMEM((2, rows, tk), F32),
                        pltpu.VMEM((rows, LANES), F32), pltpu.VMEM((rows, 2 * V_DIM), F32),
                        pltpu.VMEM((srows, K_WIDTH), F32),
                        pltpu.VMEM((2, PAGE_SIZE, K_WIDTH), F32),
                        pltpu.VMEM((srows, LANES), F32), pltpu.VMEM((srows, LANES), F32),
                        pltpu.VMEM((srows, V_DIM), F32),
                        page_buf, page_buf, pltpu.SemaphoreType.DMA((2, 2))])
    return pl.pallas_call(
        functools.partial(_attn_kernel, tq=tq, tk=tk, n_pages_step=n_pages_step,
                          steps_per_seq=steps_per_seq, n_new=n_new, lam_init=lam_init),
        grid_spec=grid_spec,
        out_shape=[jax.ShapeDtypeStruct((b, s, ATTN_WIDTH), F32),
                   jax.ShapeDtypeStruct(sq4.shape, F32)],
        compiler_params=_params(3),
        name="attention",
    )(step_pages, lp, q3, k3, v3, sq4, kn3, vn3, kt_pages, v_pages)


def _merge_kernel(x_ref, *refs, lam_init, tiles_per_seq):
    if tiles_per_seq:
        (u_ref, hist_ref, attn_ref, gate_ref, wpg_ref, ps_ref, sg_ref,
         wpp_ref, wap_ref, wo_ref, o_ref, ext_ref) = refs
        pooled = _causal_pool(u_ref, hist_ref, ext_ref,
                              lax.rem(pl.program_id(0), tiles_per_seq))
    else:
        (pooled_ref, attn_ref, gate_ref, wpg_ref, ps_ref, sg_ref,
         wpp_ref, wap_ref, wo_ref, o_ref) = refs
        pooled = [pooled_ref[:, g * POOL_GROUP_WIDTH:(g + 1) * POOL_GROUP_WIDTH]
                  for g in range(N_POOL_GROUPS)]
    mixed = [jnp.dot(pooled[g].astype(BF16), wpg_ref[g], preferred_element_type=F32)
             for g in range(N_POOL_GROUPS)]
    pool_out = jnp.concatenate(mixed, axis=1) * ps_ref[...]
    a_branch = jnp.dot(pool_out.astype(BF16), wpp_ref[...], preferred_element_type=F32)
    sg = sg_ref[...]
    heads = []
    for h in range(N_HEADS):
        oh = attn_ref[:, h * V_DIM:(h + 1) * V_DIM]
        heads.append((_rms(oh, sg) * (1.0 - lam_init)).astype(BF16))
    b_branch = jnp.dot(jnp.concatenate(heads, axis=1), wap_ref[...], preferred_element_type=F32)
    ga = gate_ref[:, :D_MODEL].astype(F32)
    gb = gate_ref[:, D_MODEL:].astype(F32)
    mix = jax.nn.sigmoid(ga) * a_branch + jax.nn.sigmoid(gb) * b_branch
    o_ref[...] = x_ref[...] + jnp.dot(mix.astype(BF16), wo_ref[...], preferred_element_type=F32)


def _merge(x2d, pool_in, attn, gates, wpg, ps, sg, wpp, wap, wo, lam_init, tm, seq_len=None):
    n = x2d.shape[0]
    row = lambda i: (i, 0)
    pool_specs = [pl.BlockSpec((tm, POOL_WIDTH), row)]
    pool_args, scratch, tiles_per_seq = [pool_in], [], 0
    if seq_len is not None:
        per = tm // HIST_ROWS
        pool_specs.append(pl.BlockSpec((HIST_ROWS, POOL_WIDTH),
                                       lambda i: (jnp.maximum(i * per - 1, 0), 0)))
        pool_args, tiles_per_seq = [pool_in, pool_in], seq_len // tm
        scratch = [pltpu.VMEM((HIST_ROWS + tm, POOL_WIDTH), F32)]
    return pl.pallas_call(
        functools.partial(_merge_kernel, lam_init=lam_init, tiles_per_seq=tiles_per_seq),
        grid=(n // tm,),
        in_specs=[pl.BlockSpec((tm, D_MODEL), row)] + pool_specs
                 + [pl.BlockSpec((tm, ATTN_WIDTH), row), pl.BlockSpec((tm, 2 * D_MODEL), row),
                    _const_spec(wpg.shape), _const_spec(ps.shape), _const_spec(sg.shape),
                    _const_spec(wpp.shape), _const_spec(wap.shape), _const_spec(wo.shape)],
        out_specs=pl.BlockSpec((tm, D_MODEL), row),
        out_shape=jax.ShapeDtypeStruct((n, D_MODEL), F32),
        scratch_shapes=scratch,
        compiler_params=_params(1),
        name="merge",
    )(x2d, *pool_args, attn, gates, wpg, ps, sg, wpp, wap, wo)


def _ffn_kernel(x_ref, nf_ref, wg_ref, wu_ref, wd_ref, fn_ref, o_ref):
    x = x_ref[...]
    h = _rms(x, nf_ref[...]).astype(BF16)
    gate = jnp.dot(h, wg_ref[...], preferred_element_type=F32)
    up = jnp.dot(h, wu_ref[...], preferred_element_type=F32)
    act = (jax.nn.silu(gate) * up).astype(BF16)
    y = x + jnp.dot(act, wd_ref[...], preferred_element_type=F32)
    o_ref[...] = _rms(y, fn_ref[...])


def _ffn(x2d, nf, wg, wu, wd, fn, tm):
    n = x2d.shape[0]
    row = lambda i: (i, 0)
    return pl.pallas_call(
        _ffn_kernel,
        grid=(n // tm,),
        in_specs=[pl.BlockSpec((tm, D_MODEL), row), _const_spec(nf.shape), _const_spec(wg.shape),
                  _const_spec(wu.shape), _const_spec(wd.shape), _const_spec(fn.shape)],
        out_specs=pl.BlockSpec((tm, D_MODEL), row),
        out_shape=jax.ShapeDtypeStruct((n, D_MODEL), F32),
        compiler_params=_params(1),
        name="ffn",
    )(x2d, nf, wg, wu, wd, fn)


def kernel(x_prompt, x_sample, cache_k, cache_v, state_pool, page_table, norm_mix, w_in,
           w_pool_group, pool_scale, lambda_q1, lambda_k1, lambda_q2, lambda_k2, subln_g,
           w_pool_proj, w_attn_proj, w_out, norm_ffn, w_gate, w_up, w_down, final_norm):
    depth = w_in.shape[0]
    assert depth == 1, "single-layer step"
    batch, seq, _ = x_prompt.shape
    dec_batch, dec_seq, _ = x_sample.shape
    n_pages = page_table.shape[1]
    past = n_pages * PAGE_SIZE
    n_pool = cache_k.shape[1]
    l = 0
    lam_init = 0.8 - 0.6 * math.exp(-0.3 * l)

    tm = ROW_TILE
    row2 = lambda a: a.reshape(1, -1)
    w_in_bf = w_in[l].astype(BF16)
    wpg = w_pool_group[l].astype(BF16)
    wpp = w_pool_proj[l].astype(BF16)
    wap = w_attn_proj[l].astype(BF16)
    wo = w_out[l].astype(BF16)
    wg = w_gate[l].astype(BF16)
    wu = w_up[l].astype(BF16)
    wd = w_down[l].astype(BF16)
    g_mix, ps, sg = row2(norm_mix[l]), row2(pool_scale[l]), row2(subln_g[l])
    nf, fn = row2(norm_ffn[l]), row2(final_norm)
    lp = jnp.stack([lambda_q1[l], lambda_k1[l], lambda_q2[l], lambda_k2[l]]).astype(F32)

    def tail(x2d, pool_in, attn, gates, tile, seq_len=None):
        x1 = _merge(x2d, pool_in, attn, gates, wpg, ps, sg, wpp, wap, wo, lam_init, tile, seq_len)
        return _ffn(x1, nf, wg, wu, wd, fn, tile)

    n_p = batch * seq
    n_s = dec_batch * dec_seq
    tabs_p = _rope_tables(jnp.arange(seq, dtype=jnp.int32))
    tabs_s = [jnp.tile(t, (dec_batch, 1))
              for t in _rope_tables(past + jnp.arange(dec_seq, dtype=jnp.int32))]
    u3, q_p, kt, kb, vf_p, vb, gates_p = _proj(x_prompt, g_mix, w_in_bf, *tabs_p, tm, True)
    u_s, q_s, kf_s, _, vf_s, _, gates_s = _proj(x_sample.reshape(1, n_s, D_MODEL), g_mix, w_in_bf,
                                                *tabs_s, n_s, False)

    q4 = q_s.reshape(dec_batch, dec_seq, N_KV_HEADS, GROUP, V_DIM).transpose(0, 2, 3, 1, 4)
    q4 = q4.reshape(dec_batch, N_KV_HEADS, GROUP * dec_seq, V_DIM).astype(F32)
    kt_pages = cache_k.transpose(0, 1, 3, 4, 5, 2).reshape(n_pool, K_WIDTH, PAGE_SIZE)
    v_pages = cache_v.reshape(n_pool, PAGE_SIZE * N_KV_HEADS, V_DIM)
    attn_p, o4 = _attention(page_table, lp, q_p, kb, vb, q4,
                            kf_s.reshape(dec_batch, dec_seq, K_WIDTH),
                            vf_s.reshape(dec_batch, dec_seq, V_WIDTH),
                            kt_pages, v_pages, lam_init, ATTN_Q_TILE, ATTN_K_TILE)
    attn_s = o4.reshape(dec_batch, N_KV_HEADS, GROUP, dec_seq, V_DIM).transpose(0, 3, 1, 2, 4)

    y_prompt = tail(x_prompt.reshape(n_p, D_MODEL), u3.reshape(n_p, POOL_WIDTH),
                    attn_p.reshape(n_p, ATTN_WIDTH), gates_p.reshape(n_p, 2 * D_MODEL), tm,
                    seq).reshape(x_prompt.shape)
    k_prompt = kt.reshape(1, batch, N_KV_HEADS, 2, HEAD_DIM, seq).transpose(0, 1, 5, 2, 3, 4)
    v_prompt = vf_p.reshape(1, batch, seq, N_KV_HEADS, V_DIM)
    pool_prompt = u3[:, seq - POOL_HIST:, :][None]

    pooled_s, pool_state = _pool_sample(
        state_pool[l].reshape(dec_batch, POOL_HIST * POOL_WIDTH),
        u_s.reshape(dec_batch, dec_seq * POOL_WIDTH), past, dec_seq)
    y_sample = tail(x_sample.reshape(n_s, D_MODEL), pooled_s.reshape(n_s, POOL_WIDTH),
                    attn_s.reshape(n_s, ATTN_WIDTH), gates_s.reshape(n_s, 2 * D_MODEL),
                    n_s).reshape(x_sample.shape)
    k_sample = kf_s.reshape(1, dec_batch, dec_seq, N_KV_HEADS, 2, HEAD_DIM)
    v_sample = vf_s.reshape(1, dec_batch, dec_seq, N_KV_HEADS, V_DIM)
    pool_sample = pool_state.reshape(1, dec_batch, POOL_HIST, POOL_WIDTH)

    return (y_prompt, y_sample, k_prompt, v_prompt, pool_prompt,
            k_sample, v_sample, pool_sample)
```
